```python
import math
import jax
import jax.numpy as jnp
from jax import lax
import numpy as np

D_MODEL = 1024
BATCH = 8
SEQ = 4096
DEPTH = 2
DEC_BATCH = 128
DEC_SEQ = 4
PAST_LEN = 16384
PAGE_SIZE = 128

HEAD_DIM = 64
Q_BLOCK = 128
ROPE_THETA = 10000.0
RMS_EPS = 1e-6
SB_HEADS = 8
SB_KV_HEADS = 2
MLA_HEADS = 8
MLA_Q_RANK = D_MODEL // 4
MLA_KV_RANK = D_MODEL // 4
MLA_NOPE_DIM = 64
MLA_ROPE_DIM = 32
MLA_V_DIM = 64
NSA_HEADS = 8
NSA_KV_GROUPS = 2
NSA_BLOCK = 64
NSA_TOPN = 16
NSA_WINDOW = 512
FOX_HEADS = 8
FOX_KV_HEADS = 2
FOX_BIAS_INIT = 3.0
MOE_GROUPS = 4
MOE_EXPERTS_PER_GROUP = 8
MOE_EXPERTS = MOE_GROUPS * MOE_EXPERTS_PER_GROUP
MOE_TOPK = 2
MOE_HIDDEN = D_MODEL // 4

N_EVEN = (DEPTH + 1) // 2
N_ODD = DEPTH // 2
SB_Q = SB_HEADS * HEAD_DIM
SB_KV = SB_KV_HEADS * HEAD_DIM
EVEN_IN = SB_Q + 2 * SB_KV + MLA_Q_RANK + MLA_KV_RANK + MLA_ROPE_DIM
EVEN_MIX = SB_Q + MLA_HEADS * MLA_V_DIM
MLA_LAT = MLA_KV_RANK + MLA_ROPE_DIM
NSA_Q = NSA_HEADS * HEAD_DIM
NSA_KV = NSA_KV_GROUPS * HEAD_DIM
FOX_Q = FOX_HEADS * HEAD_DIM
FOX_KV = FOX_KV_HEADS * HEAD_DIM
ODD_IN = NSA_Q + 6 * NSA_KV + 3 * NSA_HEADS + FOX_Q + 2 * FOX_KV + FOX_HEADS
ODD_MIX = NSA_Q + FOX_Q

kernel_name = 'hybrid_sb_mla_nsa_fox_hmoe_step'


def split_cols(x, sizes):
    cuts = np.cumsum(sizes)[:-1].tolist()
    return jnp.split(x, cuts, axis=-1)


def rmsnorm(x, g):
    x32 = x.astype(jnp.float32)
    y = x32 * lax.rsqrt(jnp.mean(x32 * x32, axis=-1, keepdims=True) + RMS_EPS)
    return (y * g.astype(jnp.float32)).astype(x.dtype)


def rope(x, pos):
    d = x.shape[-1]
    inv = ROPE_THETA ** (-jnp.arange(0, d, 2, dtype=jnp.float32) / d)
    ang = pos.astype(jnp.float32)[:, None] * inv[None, :]
    cos = jnp.cos(ang)[None, :, None, :]
    sin = jnp.sin(ang)[None, :, None, :]
    x32 = x.astype(jnp.float32)
    x1, x2 = x32[..., : d // 2], x32[..., d // 2:]
    return jnp.concatenate([x1 * cos - x2 * sin, x1 * sin + x2 * cos], axis=-1).astype(x.dtype)


def masked_softmax(s, mask):
    s = jnp.where(mask, s, -jnp.inf)
    m = jnp.max(s, axis=-1, keepdims=True)
    m = jnp.where(jnp.isfinite(m), m, 0.0)
    e = jnp.exp(s - m)
    den = jnp.sum(e, axis=-1, keepdims=True)
    return e / jnp.where(den > 0, den, 1.0)


def gather_pages(pool, i, page_table):
    g = pool[i, page_table]
    return g.reshape(g.shape[0], g.shape[1] * g.shape[2], *g.shape[3:])


def blockwise(fn, qs, qpos):
    T = qpos.shape[0]
    if T <= Q_BLOCK or T % Q_BLOCK:
        return fn(qs, qpos)
    nb = T // Q_BLOCK
    qs_b = tuple(jnp.swapaxes(q.reshape(q.shape[0], nb, Q_BLOCK, *q.shape[2:]), 0, 1) for q in qs)
    out = lax.map(lambda a: fn(a[0], a[1]), (qs_b, qpos.reshape(nb, Q_BLOCK)))
    out = jnp.swapaxes(out, 0, 1)
    return out.reshape(out.shape[0], T, *out.shape[3:])


def stick_breaking_attention(q, k, v, qpos, kpos):
    B, Tq, H, d = q.shape
    kvh = k.shape[2]
    qg = q.reshape(B, Tq, kvh, H // kvh, d)
    z = jnp.einsum('bqhgd,bkhd->bhgqk', qg, k, preferred_element_type=jnp.float32) / math.sqrt(d)
    mask = kpos[None, :] < qpos[:, None]
    log_keep = jnp.where(mask, jax.nn.log_sigmoid(-z), 0.0)
    after = lax.cumsum(log_keep, axis=4, reverse=True) - log_keep
    w = jnp.where(mask, jnp.exp(jax.nn.log_sigmoid(z) + after), 0.0).astype(v.dtype)
    return jnp.einsum('bhgqk,bkhd->bqhgd', w, v).reshape(B, Tq, H * d)


def nsa_compress(kv_all, pe, w):
    B, L, G, d = kv_all.shape
    nb = L // NSA_BLOCK
    blocks = kv_all[:, : nb * NSA_BLOCK].reshape(B, nb, NSA_BLOCK, G, d) + pe[None, None, :, None, :]
    return jnp.einsum('bnlgd,lde->bnge', blocks, w)


def to_blocks(kv_all):
    B, L, G, d = kv_all.shape
    nbs = -(-L // NSA_BLOCK)
    kv = jnp.pad(kv_all, ((0, 0), (0, nbs * NSA_BLOCK - L), (0, 0), (0, 0)))
    return kv.reshape(B, nbs, NSA_BLOCK, G, d).transpose(0, 3, 1, 2, 4)


def nsa_attention(q, gates, qpos, k_cmp, v_cmp, k_blk, v_blk, wk, wv, wpos):
    B, Tq, H, d = q.shape
    G = k_cmp.shape[2]
    hg = H // G
    scale = 1.0 / math.sqrt(d)
    qg = q.reshape(B, Tq, G, hg, d)
    nbc = k_cmp.shape[1]
    c_end = jnp.arange(nbc, dtype=jnp.int32) * NSA_BLOCK + (NSA_BLOCK - 1)
    c_mask = c_end[None, :] <= qpos[:, None]
    s_c = jnp.einsum('bqghd,bngd->bghqn', qg, k_cmp, preferred_element_type=jnp.float32) * scale
    p_c = masked_softmax(s_c, c_mask)
    o_c = jnp.einsum('bghqn,bngd->bqghd', p_c.astype(v_cmp.dtype), v_cmp)
    nbs = k_blk.shape[2]
    imp = jnp.pad(jnp.sum(p_c, axis=2), ((0, 0), (0, 0), (0, 0), (0, nbs - nbc)))
    blk = jnp.arange(nbs, dtype=jnp.int32)[None, :]
    cur = (qpos // NSA_BLOCK)[:, None]
    forced = (blk == 0) | (blk == cur) | (blk == cur - 1)
    score = jnp.where(forced, float(hg + 1), jnp.where(blk <= cur, imp, -1.0))
    n_sel = min(NSA_TOPN, nbs)
    vals, idx = lax.top_k(score, n_sel)
    valid = vals >= 0.0
    bi = jnp.arange(B)[:, None, None, None]
    gi = jnp.arange(G)[None, :, None, None]
    ks = k_blk[bi, gi, idx]
    vs = v_blk[bi, gi, idx]
    tok = idx[..., None] * NSA_BLOCK + jnp.arange(NSA_BLOCK, dtype=jnp.int32)
    s_mask = (valid[..., None] & (tok <= qpos[None, None, :, None, None])).reshape(B, G, 1, Tq, n_sel * NSA_BLOCK)
    s_s = jnp.einsum('bqghd,bgqnld->bghqnl', qg, ks, preferred_element_type=jnp.float32) * scale
    p_s = masked_softmax(s_s.reshape(B, G, hg, Tq, n_sel * NSA_BLOCK), s_mask)
    p_s = p_s.reshape(B, G, hg, Tq, n_sel, NSA_BLOCK).astype(vs.dtype)
    o_s = jnp.einsum('bghqnl,bgqnld->bqghd', p_s, vs)
    diff = qpos[:, None] - wpos[None, :]
    w_mask = (diff >= 0) & (diff < NSA_WINDOW)
    s_w = jnp.einsum('bqghd,bkgd->bghqk', qg, wk, preferred_element_type=jnp.float32) * scale
    p_w = masked_softmax(s_w, w_mask).astype(wv.dtype)
    o_w = jnp.einsum('bghqk,bkgd->bqghd', p_w, wv)
    g = gates.reshape(B, Tq, G, hg, 3)
    o = g[..., 0:1] * o_c + g[..., 1:2] * o_s + g[..., 2:3] * o_w
    return o.reshape(B, Tq, H * d)


def forgetting_attention(q, cq, k, v, ck, qpos, kpos):
    B, Tq, H, d = q.shape
    kvh = k.shape[2]
    hg = H // kvh
    qg = q.reshape(B, Tq, kvh, hg, d)
    s = jnp.einsum('bqhgd,bkhd->bhgqk', qg, k, preferred_element_type=jnp.float32) / math.sqrt(d)
    s = s + cq.reshape(B, Tq, kvh, hg).transpose(0, 2, 3, 1)[..., None] - ck[..., None, :]
    p = masked_softmax(s, kpos[None, :] <= qpos[:, None]).astype(v.dtype)
    return jnp.einsum('bhgqk,bkhd->bqhgd', p, v).reshape(B, Tq, H * d)


def even_mixer(h, pos, past, i, W):
    B, T, _ = h.shape
    sb_q, sb_k, sb_v, cq, ckv, kr = split_cols(h @ W['ev_w_in'][i], [SB_Q, SB_KV, SB_KV, MLA_Q_RANK, MLA_KV_RANK, MLA_ROPE_DIM])
    sb_q = sb_q.reshape(B, T, SB_HEADS, HEAD_DIM)
    sb_rows = jnp.stack([sb_k.reshape(B, T, SB_KV_HEADS, HEAD_DIM), sb_v.reshape(B, T, SB_KV_HEADS, HEAD_DIM)], axis=2)
    q = jnp.einsum('btc,chr->bthr', rmsnorm(cq, W['ev_q_norm'][i]), W['ev_w_uq'][i])
    q_nope, q_rope = q[..., :MLA_NOPE_DIM], rope(q[..., MLA_NOPE_DIM:], pos)
    q_lat = jnp.einsum('bthn,chn->bthc', q_nope, W['ev_w_uk'][i])
    mla_rows = jnp.concatenate([rmsnorm(ckv, W['ev_kv_norm'][i]), rope(kr[:, :, None, :], pos)[:, :, 0]], axis=-1)
    if past is None:
        sb_all, lat_all, kpos = sb_rows, mla_rows, pos
    else:
        sb_past, lat_past = past
        sb_all = jnp.concatenate([sb_past, sb_rows], axis=1)
        lat_all = jnp.concatenate([lat_past, mla_rows], axis=1)
        kpos = jnp.concatenate([jnp.arange(sb_past.shape[1], dtype=jnp.int32), pos])
    k_sb, v_sb = sb_all[:, :, 0], sb_all[:, :, 1]
    ckv_all, kr_all = lat_all[..., :MLA_KV_RANK], lat_all[..., MLA_KV_RANK:]
    w_uv = W['ev_w_uv'][i]
    mla_scale = 1.0 / math.sqrt(MLA_NOPE_DIM + MLA_ROPE_DIM)

    def block(qs, qp):
        qb_sb, qb_lat, qb_rope = qs
        Bq, Tq = qb_sb.shape[:2]
        o_sb = stick_breaking_attention(qb_sb, k_sb, v_sb, qp, kpos)
        s = (jnp.einsum('bqhc,bkc->bhqk', qb_lat, ckv_all, preferred_element_type=jnp.float32)
             + jnp.einsum('bqhr,bkr->bhqk', qb_rope, kr_all, preferred_element_type=jnp.float32)) * mla_scale
        p = masked_softmax(s, kpos[None, :] <= qp[:, None]).astype(ckv_all.dtype)
        o_lat = jnp.einsum('bhqk,bkc->bqhc', p, ckv_all)
        o_mla = jnp.einsum('bqhc,chv->bqhv', o_lat, w_uv).reshape(Bq, Tq, MLA_HEADS * MLA_V_DIM)
        return jnp.concatenate([o_sb, o_mla], axis=-1)

    out = blockwise(block, (sb_q, q_lat, q_rope), pos)
    return out @ W['ev_w_out'][i], sb_rows, mla_rows


def odd_mixer(h, pos, past, i, W):
    B, T, _ = h.shape
    G = NSA_KV_GROUPS
    nq, ck, cv, sk, sv, wk, wv, gl, fq, fk, fv, fl = split_cols(
        h @ W['od_w_in'][i],
        [NSA_Q, NSA_KV, NSA_KV, NSA_KV, NSA_KV, NSA_KV, NSA_KV, 3 * NSA_HEADS, FOX_Q, FOX_KV, FOX_KV, FOX_HEADS])

    def heads(a, n):
        return a.reshape(B, T, n, HEAD_DIM)

    nq = rope(heads(nq, NSA_HEADS), pos)
    gates = jax.nn.sigmoid(gl.astype(jnp.float32)).reshape(B, T, NSA_HEADS, 3).astype(h.dtype)
    cmp_rows = jnp.stack([rope(heads(ck, G), pos), heads(cv, G)], axis=2)
    slc_rows = jnp.stack([rope(heads(sk, G), pos), heads(sv, G)], axis=2)
    win_rows = jnp.stack([rope(heads(wk, G), pos), heads(wv, G)], axis=2)
    fq = heads(fq, FOX_HEADS)
    fox_rows = jnp.stack([heads(fk, FOX_KV_HEADS), heads(fv, FOX_KV_HEADS)], axis=2)
    logf = jax.nn.log_sigmoid((fl + W['od_fox_fb'][i]).astype(jnp.float32))
    if past is None:
        cmp_all, slc_all, fox_all, logf_all, kpos = cmp_rows, slc_rows, fox_rows, logf, pos
        win_pad = jnp.pad(win_rows, ((0, 0), (NSA_WINDOW, 0), (0, 0), (0, 0), (0, 0)))
        wpos_pad = jnp.concatenate([jnp.full((NSA_WINDOW,), -NSA_WINDOW, jnp.int32), pos])

        def window_keys(qp):
            start = qp[0] - pos[0]
            n = NSA_WINDOW + qp.shape[0]
            return (lax.dynamic_slice_in_dim(win_pad, start, n, axis=1),
                    lax.dynamic_slice_in_dim(wpos_pad, start, n, axis=0))

        win_state = win_rows[:, T - min(NSA_WINDOW, T):]
    else:
        cmp_past, slc_past, fox_past, logf_past, win_buf = past
        P = cmp_past.shape[1]
        WB = win_buf.shape[1]
        cmp_all = jnp.concatenate([cmp_past, cmp_rows], axis=1)
        slc_all = jnp.concatenate([slc_past, slc_rows], axis=1)
        fox_all = jnp.concatenate([fox_past, fox_rows], axis=1)
        logf_all = jnp.concatenate([logf_past.astype(jnp.float32), logf], axis=1)
        kpos = jnp.concatenate([jnp.arange(P, dtype=jnp.int32), pos])
        win_all = jnp.concatenate([win_buf, win_rows], axis=1)
        wpos_all = jnp.concatenate([P - WB + jnp.arange(WB, dtype=jnp.int32), pos])

        def window_keys(qp):
            return win_all, wpos_all

        win_state = win_all[:, T:]
    k_cmp = nsa_compress(cmp_all[:, :, 0], W['od_cmp_pe'][i, 0], W['od_cmp_w'][i, 0])
    v_cmp = nsa_compress(cmp_all[:, :, 1], W['od_cmp_pe'][i, 1], W['od_cmp_w'][i, 1])
    k_blk, v_blk = to_blocks(slc_all[:, :, 0]), to_blocks(slc_all[:, :, 1])
    fk_all, fv_all = fox_all[:, :, 0], fox_all[:, :, 1]
    c_all = jnp.cumsum(logf_all, axis=1)
    L = c_all.shape[1]
    c_keys = c_all.reshape(B, L, FOX_KV_HEADS, FOX_HEADS // FOX_KV_HEADS).transpose(0, 2, 3, 1)
    c_q = c_all[:, L - T:]

    def block(qs, qp):
        qb_n, gb, qb_f, cb = qs
        win_b, wpos_b = window_keys(qp)
        o_n = nsa_attention(qb_n, gb, qp, k_cmp, v_cmp, k_blk, v_blk, win_b[:, :, 0], win_b[:, :, 1], wpos_b)
        o_f = forgetting_attention(qb_f, cb, fk_all, fv_all, c_keys, qp, kpos)
        return jnp.concatenate([o_n, o_f], axis=-1)

    out = blockwise(block, (nq, gates, fq, c_q), pos)
    return out @ W['od_w_out'][i], cmp_rows, slc_rows, fox_rows, logf, win_state


def hier_moe(h, l, W):
    B, T, D = h.shape
    xt = h.reshape(B * T, D)
    N = xt.shape[0]
    p_grp = jax.nn.softmax((xt @ W['moe_w_grp'][l] + W['moe_b_grp'][l]).astype(jnp.float32), axis=-1)
    g_sel = jnp.argmax(p_grp, axis=-1)
    g_w = jnp.max(p_grp, axis=-1)
    e_logit = (xt @ W['moe_w_rt'][l] + W['moe_b_rt'][l]).astype(jnp.float32).reshape(N, MOE_GROUPS, MOE_EXPERTS_PER_GROUP)
    e_logit = jnp.take_along_axis(e_logit, g_sel[:, None, None], axis=1)[:, 0]
    top_p, top_i = lax.top_k(jax.nn.softmax(e_logit, axis=-1), MOE_TOPK)
    w = g_w[:, None] * top_p / jnp.sum(top_p, axis=-1, keepdims=True)
    eid = g_sel[:, None] * MOE_EXPERTS_PER_GROUP + top_i
    gates = jnp.sum(jax.nn.one_hot(eid, MOE_EXPERTS, dtype=jnp.float32) * w[..., None], axis=1)
    gates = gates.astype(h.dtype).reshape(N, MOE_GROUPS, MOE_EXPERTS_PER_GROUP)
    out = jnp.zeros_like(xt)
    for g in range(MOE_GROUPS):
        sl = slice(g * MOE_EXPERTS_PER_GROUP, (g + 1) * MOE_EXPERTS_PER_GROUP)
        a = jnp.einsum('nd,edf->nef', xt, W['moe_w_gate'][l, sl])
        u = jnp.einsum('nd,edf->nef', xt, W['moe_w_up'][l, sl])
        hid = jax.nn.silu(a) * u * gates[:, g, :, None]
        out = out + jnp.einsum('nef,efd->nd', hid, W['moe_w_down'][l, sl])
    return out.reshape(B, T, D)


def trunk(x, c, pos, past, W):
    B, T, D = x.shape
    names = ('sb', 'mla', 'cmp', 'slc', 'fox', 'logf', 'win')
    new = {n: [] for n in names}
    for l in range(DEPTH):
        mod = (jax.nn.silu(c) @ W['ada_w'][l] + W['ada_b'][l]).reshape(B, 6, 1, D)
        sh_a, sc_a, gt_a, sh_m, sc_m, gt_m = [mod[:, j] for j in range(6)]
        hn = rmsnorm(x, W['norm_mix'][l]) * (1 + sc_a) + sh_a
        i = l // 2
        if l % 2 == 0:
            past_l = None if past is None else (gather_pages(past['sb'], i, past['pt']),
                                                gather_pages(past['mla'], i, past['pt']))
            o, r_sb, r_mla = even_mixer(hn, pos, past_l, i, W)
            new['sb'].append(r_sb)
            new['mla'].append(r_mla)
        else:
            past_l = None if past is None else (gather_pages(past['cmp'], i, past['pt']),
                                                gather_pages(past['slc'], i, past['pt']),
                                                gather_pages(past['fox'], i, past['pt']),
                                                gather_pages(past['logf'], i, past['pt']),
                                                past['win'][i])
            o, r_cmp, r_slc, r_fox, r_logf, r_win = odd_mixer(hn, pos, past_l, i, W)
            new['cmp'].append(r_cmp)
            new['slc'].append(r_slc)
            new['fox'].append(r_fox)
            new['logf'].append(r_logf)
            new['win'].append(r_win)
        x = x + gt_a * o
        hn = rmsnorm(x, W['norm_ffn'][l]) * (1 + sc_m) + sh_m
        x = x + gt_m * hier_moe(hn, l, W)
    states = tuple(jnp.stack(new[n], axis=0) for n in names)
    return rmsnorm(x, W['norm_final']), states


def setup_inputs(seed: int = 0) -> dict:
    key = jax.random.key(seed)
    k = jax.random.split(key, 40)

    def nrm(j, shape, scale=1.0):
        return jax.random.normal(k[j], shape, jnp.float32) * scale

    n_pages = PAST_LEN // PAGE_SIZE
    n_used = DEC_BATCH * n_pages
    n_pool = n_used + n_used // 4
    wb = min(NSA_WINDOW, PAST_LEN)
    G, d = NSA_KV_GROUPS, HEAD_DIM
    page_table = jax.random.permutation(k[0], n_pool)[:n_used].reshape(DEC_BATCH, n_pages).astype(jnp.int32)
    return {
        'x_prompt': nrm(1, (BATCH, SEQ, D_MODEL)),
        'x_sample': nrm(2, (DEC_BATCH, DEC_SEQ, D_MODEL)),
        'c_prompt': nrm(3, (BATCH, D_MODEL)),
        'c_sample': nrm(4, (DEC_BATCH, D_MODEL)),
        'cache_sb_kv': nrm(5, (N_EVEN, n_pool, PAGE_SIZE, 2, SB_KV_HEADS, d)),
        'cache_mla_latent': nrm(6, (N_EVEN, n_pool, PAGE_SIZE, MLA_LAT)),
        'cache_nsa_cmp_kv': nrm(7, (N_ODD, n_pool, PAGE_SIZE, 2, G, d)),
        'cache_nsa_slc_kv': nrm(8, (N_ODD, n_pool, PAGE_SIZE, 2, G, d)),
        'cache_fox_kv': nrm(9, (N_ODD, n_pool, PAGE_SIZE, 2, FOX_KV_HEADS, d)),
        'cache_fox_logf': jax.nn.log_sigmoid(nrm(10, (N_ODD, n_pool, PAGE_SIZE, FOX_HEADS)) + FOX_BIAS_INIT),
        'state_nsa_win_kv': nrm(11, (N_ODD, DEC_BATCH, wb, 2, G, d)),
        'page_table': page_table,
        'ada_w': nrm(12, (DEPTH, D_MODEL, 6 * D_MODEL), D_MODEL ** -0.5),
        'ada_b': nrm(13, (DEPTH, 6 * D_MODEL), 0.01),
        'norm_mix': 1.0 + nrm(14, (DEPTH, D_MODEL), 0.05),
        'norm_ffn': 1.0 + nrm(15, (DEPTH, D_MODEL), 0.05),
        'norm_final': 1.0 + nrm(16, (D_MODEL,), 0.05),
        'ev_w_in': nrm(17, (N_EVEN, D_MODEL, EVEN_IN), D_MODEL ** -0.5),
        'ev_q_norm': 1.0 + nrm(18, (N_EVEN, MLA_Q_RANK), 0.05),
        'ev_kv_norm': 1.0 + nrm(19, (N_EVEN, MLA_KV_RANK), 0.05),
        'ev_w_uq': nrm(20, (N_EVEN, MLA_Q_RANK, MLA_HEADS, MLA_NOPE_DIM + MLA_ROPE_DIM), MLA_Q_RANK ** -0.5),
        'ev_w_uk': nrm(21, (N_EVEN, MLA_KV_RANK, MLA_HEADS, MLA_NOPE_DIM), MLA_KV_RANK ** -0.5),
        'ev_w_uv': nrm(22, (N_EVEN, MLA_KV_RANK, MLA_HEADS, MLA_V_DIM), MLA_KV_RANK ** -0.5),
        'ev_w_out': nrm(23, (N_EVEN, EVEN_MIX, D_MODEL), EVEN_MIX ** -0.5),
        'od_w_in': nrm(24, (N_ODD, D_MODEL, ODD_IN), D_MODEL ** -0.5),
        'od_cmp_pe': nrm(25, (N_ODD, 2, NSA_BLOCK, d), 0.1),
        'od_cmp_w': nrm(26, (N_ODD, 2, NSA_BLOCK, d, d), (NSA_BLOCK * d) ** -0.5),
        'od_fox_fb': FOX_BIAS_INIT + nrm(27, (N_ODD, FOX_HEADS), 0.5),
        'od_w_out': nrm(28, (N_ODD, ODD_MIX, D_MODEL), ODD_MIX ** -0.5),
        'moe_w_grp': nrm(29, (DEPTH, D_MODEL, MOE_GROUPS), D_MODEL ** -0.5),
        'moe_b_grp': nrm(30, (DEPTH, MOE_GROUPS), 0.01),
        'moe_w_rt': nrm(31, (DEPTH, D_MODEL, MOE_EXPERTS), D_MODEL ** -0.5),
        'moe_b_rt': nrm(32, (DEPTH, MOE_EXPERTS), 0.01),
        'moe_w_gate': nrm(33, (DEPTH, MOE_EXPERTS, D_MODEL, MOE_HIDDEN), D_MODEL ** -0.5),
        'moe_w_up': nrm(34, (DEPTH, MOE_EXPERTS, D_MODEL, MOE_HIDDEN), D_MODEL ** -0.5),
        'moe_w_down': nrm(35, (DEPTH, MOE_EXPERTS, MOE_HIDDEN, D_MODEL), MOE_HIDDEN ** -0.5),
    }


def reference(x_prompt, x_sample, c_prompt, c_sample, cache_sb_kv, cache_mla_latent, cache_nsa_cmp_kv,
              cache_nsa_slc_kv, cache_fox_kv, cache_fox_logf, state_nsa_win_kv, page_table, ada_w, ada_b,
              norm_mix, norm_ffn, norm_final, ev_w_in, ev_q_norm, ev_kv_norm, ev_w_uq, ev_w_uk, ev_w_uv,
              ev_w_out, od_w_in, od_cmp_pe, od_cmp_w, od_fox_fb, od_w_out, moe_w_grp, moe_b_grp, moe_w_rt,
              moe_b_rt, moe_w_gate, moe_w_up, moe_w_down):
    W = dict(ada_w=ada_w, ada_b=ada_b, norm_mix=norm_mix, norm_ffn=norm_ffn, norm_final=norm_final,
             ev_w_in=ev_w_in, ev_q_norm=ev_q_norm, ev_kv_norm=ev_kv_norm, ev_w_uq=ev_w_uq, ev_w_uk=ev_w_uk,
             ev_w_uv=ev_w_uv, ev_w_out=ev_w_out, od_w_in=od_w_in, od_cmp_pe=od_cmp_pe, od_cmp_w=od_cmp_w,
             od_fox_fb=od_fox_fb, od_w_out=od_w_out, moe_w_grp=moe_w_grp, moe_b_grp=moe_b_grp,
             moe_w_rt=moe_w_rt, moe_b_rt=moe_b_rt, moe_w_gate=moe_w_gate, moe_w_up=moe_w_up,
             moe_w_down=moe_w_down)
    past_len = page_table.shape[1] * cache_sb_kv.shape[2]
    pos_p = jnp.arange(x_prompt.shape[1], dtype=jnp.int32)
    pos_s = past_len + jnp.arange(x_sample.shape[1], dtype=jnp.int32)
    past = dict(pt=page_table, sb=cache_sb_kv, mla=cache_mla_latent, cmp=cache_nsa_cmp_kv, slc=cache_nsa_slc_kv,
                fox=cache_fox_kv, logf=cache_fox_logf, win=state_nsa_win_kv)
    y_prompt, (sb_p, mla_p, cmp_p, slc_p, fox_p, logf_p, win_p) = trunk(x_prompt, c_prompt, pos_p, None, W)
    y_sample, (sb_s, mla_s, cmp_s, slc_s, fox_s, logf_s, win_s) = trunk(x_sample, c_sample, pos_s, past, W)
    return (y_prompt, y_sample, sb_p, sb_s, mla_p, mla_s, cmp_p, cmp_s, slc_p, slc_s, fox_p, fox_s, logf_p, logf_s, win_p, win_s)
```

```python
import functools
import math

import jax
import jax.numpy as jnp
import numpy as np
from jax import lax
from jax.experimental import pallas as pl
from jax.experimental.pallas import tpu as pltpu

F32 = jnp.float32
BF = jnp.bfloat16
HI = lax.Precision.HIGHEST

HEAD_DIM = 64
ROPE_THETA = 10000.0
RMS_EPS = 1e-6
SB_HEADS, SB_KV_HEADS = 8, 2
MLA_HEADS, MLA_NOPE, MLA_ROPE, MLA_V = 8, 64, 32, 64
NSA_HEADS, NSA_GROUPS, NSA_BLOCK, NSA_TOPN, NSA_WINDOW = 8, 2, 64, 16, 512
FOX_HEADS, FOX_KV_HEADS = 8, 2
MOE_GROUPS, MOE_PER_GROUP, MOE_TOPK = 4, 8, 2
MOE_EXPERTS = MOE_GROUPS * MOE_PER_GROUP
MLA_SCALE = 1.0 / math.sqrt(MLA_NOPE + MLA_ROPE)
QK_SCALE = 1.0 / math.sqrt(HEAD_DIM)
LANES = 128
VMEM_LIMIT_MB = 56


def _cparams(sem, vmem_mb=VMEM_LIMIT_MB):
    return pltpu.CompilerParams(dimension_semantics=sem, vmem_limit_bytes=vmem_mb * 2 ** 20)


def _round_up(a, m):
    return (a + m - 1) // m * m


def _pick_tile(n, pref):
    t = min(pref, n)
    while n % t:
        t //= 2
    return t


def _rms(x, g):
    return x * lax.rsqrt(jnp.mean(x * x, axis=-1, keepdims=True) + RMS_EPS) * g


def _dot(a, b):
    return jnp.dot(a, b, preferred_element_type=F32)


def _dot_t(a, b, precision=None):
    return lax.dot_general(a, b, (((1,), (1,)), ((), ())), preferred_element_type=F32, precision=precision)


def _split3(x):
    a = x.astype(BF)
    r = x - a.astype(F32)
    b = r.astype(BF)
    c = (r - b.astype(F32)).astype(BF)
    return a, b, c


def _linear_body(*refs, act, mode, has_bias):
    x_ref, w_ref = refs[0], refs[1]
    b_ref = refs[2] if has_bias else None
    o_ref = refs[-1]
    x = x_ref[...]
    if act == "silu":
        x = x.astype(F32)
        x = x * jax.nn.sigmoid(x)
    w = w_ref[...]
    if mode == "hi":
        y = jnp.dot(x.astype(F32), w.astype(F32), precision=HI, preferred_element_type=F32)
    elif mode == "split3":
        a, b, c = _split3(x.astype(F32))
        wb = w.astype(BF)
        y = _dot(a, wb) + _dot(b, wb) + _dot(c, wb)
    else:
        y = _dot(x.astype(BF), w.astype(BF))
    if has_bias:
        y = y + b_ref[...]
    o_ref[...] = y.astype(o_ref.dtype)


def linear(x, w, bias=None, *, widx=None, act=None, mode="bf16", out_dtype=F32, tm=256, tn=512, name="linear"):
    M, K = x.shape
    N = w.shape[-1]
    tm = _pick_tile(M, tm)
    tn = _pick_tile(N, tn)
    if w.ndim == 3:
        w_spec = pl.BlockSpec((None, K, tn), lambda i, j: (widx, 0, j))
    else:
        w_spec = pl.BlockSpec((K, tn), lambda i, j: (0, j))
    in_specs = [pl.BlockSpec((tm, K), lambda i, j: (i, 0)), w_spec]
    args = [x, w]
    if bias is not None:
        if bias.ndim == 3:
            in_specs.append(pl.BlockSpec((None, 1, tn), lambda i, j: (widx, 0, j)))
        else:
            in_specs.append(pl.BlockSpec((1, tn), lambda i, j: (0, j)))
        args.append(bias)
    return pl.pallas_call(
        functools.partial(_linear_body, act=act, mode=mode, has_bias=bias is not None),
        grid=(M // tm, N // tn),
        in_specs=in_specs,
        out_specs=pl.BlockSpec((tm, tn), lambda i, j: (i, j)),
        out_shape=jax.ShapeDtypeStruct((M, N), out_dtype),
        compiler_params=_cparams(("parallel", "parallel")),
        name=name,
    )(*args)


def _mod_spec(tm, D, mod_rows):
    return lambda R: pl.BlockSpec((None, R, D), lambda i: ((i * tm) // mod_rows, 0, 0))


def _even_front_body(x_ref, g_ref, sc_ref, sh_ref, w_ref, c32_ref, s32_ref, c256_ref, s256_ref, gq_ref, gkv_ref,
                     wuq_ref, wa_ref, wb_ref, sbq_ref, sbrows_ref, mla_ref, o1_ref, o2_ref, o3_ref, o4_ref, *, absorbed):
    hn = _rms(x_ref[...], g_ref[...]) * (1.0 + sc_ref[...]) + sh_ref[...]
    proj = _dot(hn.astype(BF), w_ref[...])
    sbq_ref[...] = (proj[:, :512] * QK_SCALE).astype(BF)
    sbrows_ref[...] = proj[:, 512:768]
    cqn = _rms(proj[:, 768:1024], gq_ref[...])
    ckvn = _rms(proj[:, 1024:1280], gkv_ref[...])
    krr = proj[:, 1280:1408] * c32_ref[...] + proj[:, 1408:1536] * s32_ref[...]
    mla_ref[:, :256] = ckvn
    mla_ref[:, 256:288] = krr[:, :32]
    qu = _dot(cqn.astype(BF), wuq_ref[...])
    qr = (qu[:, 512:768] * c256_ref[...] + qu[:, 768:1024] * s256_ref[...]) * MLA_SCALE
    qn = qu[:, :512]
    o2_ref[...] = qr.astype(BF)
    if absorbed:
        o1_ref[...] = (_dot(qn.astype(BF), wa_ref[...]) * MLA_SCALE).astype(BF)
    else:
        o1_ref[...] = (qn * MLA_SCALE).astype(BF)
        ckb = ckvn.astype(BF)
        o3_ref[...] = _dot(ckb, wa_ref[...]).astype(BF)
        o4_ref[...] = _dot(ckb, wb_ref[...]).astype(BF)


def even_front(x2, g, sc, sh, wext, tabs, gq, gkv, wuq, wa, wb, *, tm, mod_rows, tab_rows, absorbed):
    N, D = x2.shape
    R = sc.shape[1]
    c32, s32, c256, s256 = tabs
    nt = tab_rows // tm
    row = lambda w: pl.BlockSpec((tm, w), lambda i: (i, 0))
    tab = lambda w: pl.BlockSpec((tm, w), lambda i: (i % nt, 0))
    full = lambda a: pl.BlockSpec(a.shape, lambda i: (0,) * a.ndim)
    mod = _mod_spec(tm, D, mod_rows)(R)
    in_specs = [row(D), full(g), mod, mod, full(wext), tab(128), tab(128), tab(256), tab(256), full(gq), full(gkv),
                full(wuq), full(wa)]
    args = [x2, g, sc, sh, wext, c32, s32, c256, s256, gq, gkv, wuq, wa]
    if absorbed:
        body = lambda *r: _even_front_body(*r[:13], None, *r[13:], None, None, absorbed=True)
        outs = [(512, BF), (256, F32), (288, F32), (2048, BF), (256, BF)]
    else:
        in_specs.append(full(wb))
        args.append(wb)
        body = functools.partial(_even_front_body, absorbed=False)
        outs = [(512, BF), (256, F32), (288, F32), (512, BF), (256, BF), (512, BF), (512, BF)]
    return pl.pallas_call(
        body,
        grid=(N // tm,),
        in_specs=in_specs,
        out_specs=[row(w) for w, _ in outs],
        out_shape=[jax.ShapeDtypeStruct((N, w), dt) for w, dt in outs],
        compiler_params=_cparams(("parallel",)),
        name="even_front",
    )(*args)


def _log_sigmoid(z):
    return jnp.minimum(z, 0.0) - jnp.log(1.0 + jnp.exp(-jnp.abs(z)))


def _odd_front_body(x_ref, g_ref, sc_ref, sh_ref, w_ref, c512_ref, s512_ref, c128_ref, s128_ref, fb_ref,
                    nq_ref, nqf_ref, cmp_ref, slc_ref, win_ref, gates_ref, fq_ref, fox_ref, logf_ref):
    hn = _rms(x_ref[...], g_ref[...]) * (1.0 + sc_ref[...]) + sh_ref[...]
    p = _dot(hn.astype(BF), w_ref[...])
    nq = (p[:, 0:512] * c512_ref[...] + p[:, 512:1024] * s512_ref[...]) * QK_SCALE
    nq_ref[...] = nq.astype(BF)
    nqf_ref[...] = nq
    c128, s128 = c128_ref[...], s128_ref[...]
    cmp_ref[:, :128] = p[:, 1024:1152] * c128 + p[:, 1152:1280] * s128
    cmp_ref[:, 128:] = p[:, 1280:1408]
    slc_ref[:, :128] = p[:, 1408:1536] * c128 + p[:, 1536:1664] * s128
    slc_ref[:, 128:] = p[:, 1664:1792]
    win_ref[:, :128] = p[:, 1792:1920] * c128 + p[:, 1920:2048] * s128
    win_ref[:, 128:] = p[:, 2048:2176]
    gates_ref[...] = jax.nn.sigmoid(p[:, 2176:2304])
    fq_ref[...] = (p[:, 2304:2816] * QK_SCALE).astype(BF)
    fox_ref[...] = p[:, 2816:3072]
    logf_ref[...] = _log_sigmoid(p[:, 3072:3200] + fb_ref[...])


def odd_front(x2, g, sc, sh, wext, tabs, fb, *, tm, mod_rows, tab_rows):
    N, D = x2.shape
    R = sc.shape[1]
    nt = tab_rows // tm
    row = lambda w: pl.BlockSpec((tm, w), lambda i: (i, 0))
    tab = lambda w: pl.BlockSpec((tm, w), lambda i: (i % nt, 0))
    full = lambda a: pl.BlockSpec(a.shape, lambda i: (0,) * a.ndim)
    mod = _mod_spec(tm, D, mod_rows)(R)
    outs = [(512, BF), (512, F32), (256, F32), (256, F32), (256, F32), (128, F32), (512, BF), (256, F32), (128, F32)]
    return pl.pallas_call(
        _odd_front_body,
        grid=(N // tm,),
        in_specs=[row(D), full(g), mod, mod, full(wext), tab(512), tab(512), tab(128), tab(128), full(fb)],
        out_specs=[row(w) for w, _ in outs],
        out_shape=[jax.ShapeDtypeStruct((N, w), dt) for w, dt in outs],
        compiler_params=_cparams(("parallel",)),
        name="odd_front",
    )(x2, g, sc, sh, wext, *tabs, fb)


def _outproj_body(mix_ref, w_ref, x_ref, gt_ref, o_ref):
    o_ref[...] = x_ref[...] + gt_ref[...] * _dot(mix_ref[...].astype(BF), w_ref[...])


def out_proj_res(mix, w, x2, gt, *, tm, mod_rows):
    N, D = x2.shape
    K = mix.shape[1]
    return pl.pallas_call(
        _outproj_body,
        grid=(N // tm,),
        in_specs=[pl.BlockSpec((tm, K), lambda i: (i, 0)), pl.BlockSpec(w.shape, lambda i: (0, 0)),
                  pl.BlockSpec((tm, D), lambda i: (i, 0)), _mod_spec(tm, D, mod_rows)(gt.shape[1])],
        out_specs=pl.BlockSpec((tm, D), lambda i: (i, 0)),
        out_shape=jax.ShapeDtypeStruct((N, D), F32),
        compiler_params=_cparams(("parallel",)),
        name="out_proj_res",
    )(mix, w, x2, gt)


def _nsa_combine_body(oc_ref, os_ref, ow_ref, g_ref, e_ref, of_ref, o_ref):
    a, b, c = _split3(g_ref[...])
    e = e_ref[...]
    gx = _dot(a, e) + _dot(b, e) + _dot(c, e)
    o = (gx[:, :512] * oc_ref[...].astype(F32) + gx[:, 512:1024] * os_ref[...].astype(F32)
         + gx[:, 1024:] * ow_ref[...].astype(F32))
    o_ref[:, :512] = o.astype(BF)
    o_ref[:, 512:] = of_ref[...]


def nsa_combine(oc, osel, ow, gates, expand, of, *, tm):
    N = oc.shape[0]
    row = lambda w: pl.BlockSpec((tm, w), lambda i: (i, 0))
    return pl.pallas_call(
        _nsa_combine_body,
        grid=(N // tm,),
        in_specs=[row(512), row(512), row(512), row(128), pl.BlockSpec(expand.shape, lambda i: (0, 0)), row(512)],
        out_specs=row(1024),
        out_shape=jax.ShapeDtypeStruct((N, 1024), BF),
        compiler_params=_cparams(("parallel",)),
        name="nsa_combine",
    )(oc, osel, ow, gates, expand, of)


def _final_norm_body(x_ref, g_ref, o_ref):
    o_ref[...] = _rms(x_ref[...], g_ref[...])


def final_norm(x2, g, *, tm):
    N, D = x2.shape
    return pl.pallas_call(
        _final_norm_body,
        grid=(N // tm,),
        in_specs=[pl.BlockSpec((tm, D), lambda i: (i, 0)), pl.BlockSpec((1, D), lambda i: (0, 0))],
        out_specs=pl.BlockSpec((tm, D), lambda i: (i, 0)),
        out_shape=jax.ShapeDtypeStruct((N, D), F32),
        compiler_params=_cparams(("parallel",)),
        name="final_norm",
    )(x2, g)


def _moe_body(x_ref, g_ref, sc_ref, sh_ref, gt_ref, wgrp_ref, bgrp_ref, wrt_ref, brt_ref, wg_ref, wu_ref, wd_ref,
              o_ref, hn_s, gates_s, acc_s):
    e = pl.program_id(1)
    lane = lax.broadcasted_iota(jnp.int32, gates_s.shape, 1)
    lanef = lane.astype(F32)

    @pl.when(e == 0)
    def _():
        hn = _rms(x_ref[...], g_ref[...]) * (1.0 + sc_ref[...]) + sh_ref[...]
        hn_s[...] = hn.astype(BF)
        lg = jnp.dot(hn, wgrp_ref[...], precision=HI, preferred_element_type=F32) + bgrp_ref[...]
        lr = jnp.dot(hn, wrt_ref[...], precision=HI, preferred_element_type=F32) + brt_ref[...]
        lg = jnp.where(lane < MOE_GROUPS, lg, -jnp.inf)
        mg = jnp.max(lg, axis=1, keepdims=True)
        g_sel = jnp.min(jnp.where(lg == mg, lanef, 1e9), axis=1, keepdims=True)
        g_w = 1.0 / jnp.sum(jnp.exp(lg - mg), axis=1, keepdims=True)
        ingrp = (lane >> 3).astype(F32) == g_sel
        ingrp = ingrp & (lane < MOE_EXPERTS)
        lrm = jnp.where(ingrp, lr, -jnp.inf)
        m1 = jnp.max(lrm, axis=1, keepdims=True)
        ex = jnp.exp(lrm - m1)
        i1 = jnp.min(jnp.where(ingrp & (ex == 1.0), lanef, 1e9), axis=1, keepdims=True)
        c2 = jnp.where(ingrp & (lanef != i1), ex, -1.0)
        v2 = jnp.max(c2, axis=1, keepdims=True)
        i2 = jnp.min(jnp.where(c2 == v2, lanef, 1e9), axis=1, keepdims=True)
        den = 1.0 + v2
        gates_s[...] = jnp.where(lanef == i1, g_w / den, 0.0) + jnp.where(lanef == i2, g_w * v2 / den, 0.0)
        acc_s[...] = jnp.zeros_like(acc_s)

    ge = jnp.sum(jnp.where(lane == e, gates_s[...], 0.0), axis=1, keepdims=True)
    hb = hn_s[...]
    a = _dot(hb, wg_ref[...])
    u = _dot(hb, wu_ref[...])
    hid = a * jax.nn.sigmoid(a) * u * ge
    acc_s[...] += _dot(hid.astype(BF), wd_ref[...])

    @pl.when(e == pl.num_programs(1) - 1)
    def _():
        o_ref[...] = x_ref[...] + gt_ref[...] * acc_s[...]


def moe_layer(x2, g, sc, sh, gt, wgrp, bgrp, wrt, brt, wg, wu, wd, l, *, tm, mod_rows):
    N, D = x2.shape
    Hh = wg.shape[-1]
    row = pl.BlockSpec((tm, D), lambda i, e: (i, 0))
    mod = lambda a: pl.BlockSpec((None, a.shape[1], D), lambda i, e: ((i * tm) // mod_rows, 0, 0))
    full = lambda a: pl.BlockSpec(a.shape, lambda i, e: (0,) * a.ndim)
    return pl.pallas_call(
        _moe_body,
        grid=(N // tm, MOE_EXPERTS),
        in_specs=[row, full(g), mod(sc), mod(sh), mod(gt), full(wgrp), full(bgrp), full(wrt), full(brt),
                  pl.BlockSpec((None, None, D, Hh), lambda i, e: (l, e, 0, 0)),
                  pl.BlockSpec((None, None, D, Hh), lambda i, e: (l, e, 0, 0)),
                  pl.BlockSpec((None, None, Hh, D), lambda i, e: (l, e, 0, 0))],
        out_specs=row,
        out_shape=jax.ShapeDtypeStruct((N, D), F32),
        scratch_shapes=[pltpu.VMEM((tm, D), BF), pltpu.VMEM((tm, LANES), F32), pltpu.VMEM((tm, D), F32)],
        compiler_params=_cparams(("parallel", "arbitrary")),
        name="moe",
    )(x2, g, sc, sh, gt, wgrp, bgrp, wrt, brt, wg, wu, wd)


def _softplus(z):
    return jnp.maximum(z, 0.0) + jnp.log(1.0 + jnp.exp(-jnp.abs(z)))


def _online_softmax_step(s, v, m_s, l_s, acc_s):
    m_prev = m_s[:, :1]
    m_new = jnp.maximum(m_prev, jnp.max(s, axis=1, keepdims=True))
    m_use = jnp.where(m_new > -jnp.inf, m_new, 0.0)
    p = jnp.exp(s - m_use)
    alpha = jnp.exp(m_prev - m_use)
    l_s[...] = jnp.broadcast_to(alpha * l_s[:, :1] + jnp.sum(p, axis=1, keepdims=True), l_s.shape)
    acc_s[...] = alpha * acc_s[...] + _dot(p.astype(BF), v)
    m_s[...] = jnp.broadcast_to(m_new, m_s.shape)


def _pflash_body(*refs, mode, hg, t, nk, window):
    it = iter(refs)
    q_ref, k_ref, v_ref = next(it), next(it), next(it)
    cq_ref = ck_ref = sel_ref = e_ref = u_ref = None
    if mode == "fox":
        cq_ref, ck_ref = next(it), next(it)
    if mode == "sel":
        sel_ref, e_ref = next(it), next(it)
    if mode == "sb":
        u_ref = next(it)
    o_ref, m_s, l_s, acc_s = next(it), next(it), next(it), next(it)
    i, j = pl.program_id(2), pl.program_id(3)
    R = hg * t

    @pl.when(j == 0)
    def _():
        m_s[...] = jnp.full(m_s.shape, 0.0 if mode == "sb" else -jnp.inf, F32)
        l_s[...] = jnp.zeros_like(l_s)
        acc_s[...] = jnp.zeros_like(acc_s)

    if mode == "win":
        kt = i - (nk - 1) + j
        active = kt >= 0
    elif mode == "sb":
        kt = i - j
        active = j <= i
    else:
        kt = j
        active = j <= i

    @pl.when(active)
    def _():
        q = q_ref[...].reshape(R, q_ref.shape[-1])
        s3 = _dot_t(q, k_ref[...]).reshape(hg, t, t)
        qpos = i * t + lax.broadcasted_iota(jnp.int32, (t, t), 0)
        kpos = kt * t + lax.broadcasted_iota(jnp.int32, (t, t), 1)
        if mode == "sb":
            mask = kpos < qpos
        elif mode == "win":
            d = qpos - kpos
            mask = (d >= 0) & (d < window)
        else:
            mask = kpos <= qpos
        if mode == "sel":
            mask = mask & (_dot(sel_ref[...], e_ref[...]) > 0.5)
        if mode == "fox":
            s3 = s3 + cq_ref[...][:, :, :1] - ck_ref[...]
        if mode == "sb":
            z3 = s3
            lk3 = jnp.where(mask[None], -_softplus(z3), 0.0)
            lk = lk3.reshape(R, t)
            hi = lk.astype(BF)
            lo = (lk - hi.astype(F32)).astype(BF)
            u = u_ref[...]
            suf = _dot(hi, u) + _dot(lo, u)
            carry = m_s[:, :1]
            w3 = jnp.where(mask[None], jnp.exp(z3 + lk3 + (suf + carry).reshape(hg, t, 1 * t)), 0.0)
            acc_s[...] += _dot(w3.reshape(R, t).astype(BF), v_ref[...])
            m_s[...] = jnp.broadcast_to(carry + jnp.sum(lk, axis=1, keepdims=True), m_s.shape)
        else:
            s = jnp.where(mask[None], s3, -jnp.inf).reshape(R, t)
            _online_softmax_step(s, v_ref[...], m_s, l_s, acc_s)

    @pl.when(j == nk - 1)
    def _():
        if mode == "sb":
            out = acc_s[...]
        else:
            l = l_s[:, :1]
            out = acc_s[...] / jnp.where(l > 0.0, l, 1.0)
        o_ref[...] = out.reshape(o_ref.shape).astype(o_ref.dtype)


def pflash(q, k, v, *, mode, t, extras=(), window=NSA_WINDOW):
    B, G, hg, T, dk = q.shape
    dv = v.shape[-1]
    t = _pick_tile(T, t)
    nq = T // t
    nk = (window // t + 1) if mode == "win" else nq
    if mode == "win":
        assert window % t == 0
        kidx = lambda i, j: jnp.maximum(i - (nk - 1) + j, 0)
    elif mode == "sb":
        kidx = lambda i, j: jnp.maximum(i - j, 0)
    else:
        kidx = lambda i, j: jnp.minimum(j, i)
    in_specs = [pl.BlockSpec((None, None, hg, t, dk), lambda b, g, i, j: (b, g, 0, i, 0)),
                pl.BlockSpec((None, None, t, dk), lambda b, g, i, j: (b, g, kidx(i, j), 0)),
                pl.BlockSpec((None, None, t, dv), lambda b, g, i, j: (b, g, kidx(i, j), 0))]
    if mode == "fox":
        in_specs += [pl.BlockSpec((None, None, hg, t, LANES), lambda b, g, i, j: (b, g, 0, i, 0)),
                     pl.BlockSpec((None, None, hg, 1, t), lambda b, g, i, j: (b, g, 0, 0, kidx(i, j)))]
    if mode == "sel":
        nbp = extras[0].shape[-1]
        in_specs += [pl.BlockSpec((None, None, t, nbp), lambda b, g, i, j: (b, g, i, 0)),
                     pl.BlockSpec((nbp, t), lambda b, g, i, j: (0, kidx(i, j)))]
    if mode == "sb":
        in_specs += [pl.BlockSpec((t, t), lambda b, g, i, j: (0, 0))]
    R = hg * t
    return pl.pallas_call(
        functools.partial(_pflash_body, mode=mode, hg=hg, t=t, nk=nk, window=window),
        grid=(B, G, nq, nk),
        in_specs=in_specs,
        out_specs=pl.BlockSpec((None, None, hg, t, dv), lambda b, g, i, j: (b, g, 0, i, 0)),
        out_shape=jax.ShapeDtypeStruct((B, G, hg, T, dv), BF),
        scratch_shapes=[pltpu.VMEM((R, LANES), F32), pltpu.VMEM((R, LANES), F32), pltpu.VMEM((R, dv), F32)],
        compiler_params=_cparams(("parallel", "parallel", "parallel", "arbitrary")),
        name="pflash_" + mode,
    )(q, k, v, *extras)


def _cmpsel_body(q_ref, kc_ref, vc_ref, oc_ref, sel_ref, *, hg, tq, nbc, nbs, n_sel, pos0):
    qi = pl.program_id(2)
    R = hg * tq
    nbp = kc_ref.shape[0]
    s = _dot_t(q_ref[...], kc_ref[...], precision=HI)
    n_r = lax.broadcasted_iota(jnp.int32, (R, nbp), 1)
    r_r = lax.broadcasted_iota(jnp.int32, (R, nbp), 0)
    qpos_r = pos0 + qi * tq + (r_r % tq)
    cmask = (n_r * NSA_BLOCK + (NSA_BLOCK - 1) <= qpos_r) & (n_r < nbc)
    s = jnp.where(cmask, s, -jnp.inf)
    m = jnp.max(s, axis=1, keepdims=True)
    m = jnp.where(m > -jnp.inf, m, 0.0)
    ex = jnp.exp(s - m)
    den = jnp.sum(ex, axis=1, keepdims=True)
    p = ex / jnp.where(den > 0.0, den, 1.0)
    oc_ref[...] = _dot(p.astype(BF), vc_ref[...].astype(BF)).astype(oc_ref.dtype)
    tt = lax.broadcasted_iota(jnp.int32, (tq, R), 0)
    rr = lax.broadcasted_iota(jnp.int32, (tq, R), 1)
    fold = (rr % tq == tt).astype(BF)
    a, b, c = _split3(p)
    imp = _dot(fold, a) + _dot(fold, b) + _dot(fold, c)
    n = lax.broadcasted_iota(jnp.int32, (tq, nbp), 1)
    qpos = pos0 + qi * tq + lax.broadcasted_iota(jnp.int32, (tq, nbp), 0)
    cur = qpos // NSA_BLOCK
    forced = (n == 0) | (n == cur) | (n == cur - 1)
    score = jnp.where(forced, float(hg + 1), jnp.where(n <= cur, imp, -1.0))
    score = jnp.where(n < nbs, score, -2.0)
    nf = n.astype(F32)
    sel = jnp.zeros((tq, nbp), F32)
    for _ in range(n_sel):
        mx = jnp.max(score, axis=1, keepdims=True)
        idx = jnp.min(jnp.where(score == mx, nf, 1e9), axis=1, keepdims=True)
        hit = nf == idx
        sel = jnp.where(hit & (mx >= 0.0), 1.0, sel)
        score = jnp.where(hit, -3.0, score)
    sel_ref[...] = sel.astype(sel_ref.dtype)


def nsa_cmp_sel(qf, kc, vc, *, tq, nbc, nbs, pos0):
    B, G, nqt, R, d = qf.shape
    hg = R // tq
    nbp = kc.shape[2]
    n_sel = min(NSA_TOPN, nbs)
    return pl.pallas_call(
        functools.partial(_cmpsel_body, hg=hg, tq=tq, nbc=nbc, nbs=nbs, n_sel=n_sel, pos0=pos0),
        grid=(B, G, nqt),
        in_specs=[pl.BlockSpec((None, None, None, R, d), lambda b, g, i: (b, g, i, 0, 0)),
                  pl.BlockSpec((None, None, nbp, d), lambda b, g, i: (b, g, 0, 0)),
                  pl.BlockSpec((None, None, nbp, d), lambda b, g, i: (b, g, 0, 0))],
        out_specs=[pl.BlockSpec((None, None, None, R, d), lambda b, g, i: (b, g, i, 0, 0)),
                   pl.BlockSpec((None, None, None, tq, nbp), lambda b, g, i: (b, g, i, 0, 0))],
        out_shape=[jax.ShapeDtypeStruct((B, G, nqt, R, d), BF), jax.ShapeDtypeStruct((B, G, nqt, tq, nbp), BF)],
        compiler_params=_cparams(("parallel", "parallel", "parallel")),
        name="nsa_cmp_sel",
    )(qf, kc, vc)


PAGE = 128
NEW_PAD = PAGE


def _dec_softmax_step(s, vs, m_s, l_s, acc_s):
    m_prev = m_s[:, :1]
    m_new = jnp.maximum(m_prev, jnp.max(s, axis=1, keepdims=True))
    m_use = jnp.where(m_new > -jnp.inf, m_new, 0.0)
    p = jnp.exp(s - m_use)
    alpha = jnp.exp(m_prev - m_use)
    l_s[...] = jnp.broadcast_to(alpha * l_s[:, :1] + jnp.sum(p, axis=1, keepdims=True), l_s.shape)
    w = vs[0].shape[0]
    pv = _dot(p[:, :w].astype(BF), vs[0])
    for n in range(1, len(vs)):
        pv = pv + _dot(p[:, n * w:(n + 1) * w].astype(BF), vs[n])
    acc_s[...] = alpha * acc_s[...] + pv
    m_s[...] = jnp.broadcast_to(m_new, m_s.shape)


def _dec_body(pt_ref, *refs, mode, G, nc, n_new, rows_per_t, t_period, wlen, window):
    del pt_ref
    it = iter(refs)
    q_ref = next(it)
    cq_ref = ck_ref = cknew_ref = selx_ref = e2_ref = selnew_ref = u_ref = None
    if mode == "fox":
        cq_ref, ck_ref, cknew_ref = next(it), next(it), next(it)
    if mode == "sel":
        selx_ref, e2_ref, selnew_ref = next(it), next(it), next(it)
    if mode == "sb":
        u_ref = next(it)
    pages = [next(it) for _ in range(G)]
    new_ref = next(it)
    o_ref, m_s, l_s, acc_s = next(it), next(it), next(it), next(it)
    c = pl.program_id(1)
    q = q_ref[...]
    R = q.shape[0]

    def row_t(shape):
        r = lax.broadcasted_iota(jnp.int32, shape, 0)
        return (r % t_period) // rows_per_t

    def split_kv(x):
        if mode == "mla":
            kb = x.astype(BF)
            return kb, kb[:, :256]
        return x[:, :128].astype(BF), x[:, 128:].astype(BF)

    @pl.when(c == 0)
    def _():
        m_s[...] = jnp.full(m_s.shape, 0.0 if mode == "sb" else -jnp.inf, F32)
        l_s[...] = jnp.zeros_like(l_s)
        acc_s[...] = jnp.zeros_like(acc_s)
        kb, vb = split_kv(new_ref[...])
        s = _dot_t(q, kb)
        t = row_t(s.shape)
        key = lax.broadcasted_iota(jnp.int32, s.shape, 1)
        if mode == "sb":
            mask = (key < t) & (key < n_new)
            lk = jnp.where(mask, -_softplus(s), 0.0)
            hi = lk.astype(BF)
            lo = (lk - hi.astype(F32)).astype(BF)
            suf = _dot(hi, u_ref[...]) + _dot(lo, u_ref[...])
            w = jnp.where(mask, jnp.exp(s + lk + suf), 0.0)
            acc_s[...] += _dot(w.astype(BF), vb)
            m_s[...] = jnp.broadcast_to(jnp.sum(lk, axis=1, keepdims=True), m_s.shape)
        else:
            mask = (key <= t) & (key < n_new)
            if mode == "sel":
                mask = mask & (selnew_ref[...] > 0.5)
            if mode == "fox":
                s = s + cq_ref[...][:, :1] - cknew_ref[...]
            _dec_softmax_step(jnp.where(mask, s, -jnp.inf), [vb], m_s, l_s, acc_s)

    kvs = [split_kv(p[...]) for p in pages]
    zs = [_dot_t(q, kb) for kb, _ in kvs]
    if mode == "sb":
        u = u_ref[...]
        carry = m_s[:, :1]
        pv = None
        for z, (_, vb) in zip(zs, kvs):
            lk = -_softplus(z)
            hi = lk.astype(BF)
            lo = (lk - hi.astype(F32)).astype(BF)
            suf = _dot(hi, u) + _dot(lo, u)
            w = jnp.exp(z + lk + suf + carry)
            d = _dot(w.astype(BF), vb)
            pv = d if pv is None else pv + d
            carry = carry + jnp.sum(lk, axis=1, keepdims=True)
        acc_s[...] += pv
        m_s[...] = jnp.broadcast_to(carry, m_s.shape)
    else:
        s = jnp.concatenate(zs, axis=1) if G > 1 else zs[0]
        if mode == "fox":
            s = s + cq_ref[...][:, :1] - ck_ref[...]
        if mode == "sel":
            s = jnp.where(_dot(selx_ref[...], e2_ref[...]) > 0.5, s, -jnp.inf)
        if mode == "win":
            widx = c * (G * PAGE) + lax.broadcasted_iota(jnp.int32, s.shape, 1)
            s = jnp.where(widx > row_t(s.shape) + (wlen - window), s, -jnp.inf)
        _dec_softmax_step(s, [vb for _, vb in kvs], m_s, l_s, acc_s)

    @pl.when(c == nc - 1)
    def _():
        if mode == "sb":
            o_ref[...] = acc_s[...]
        else:
            l = l_s[:, :1]
            o_ref[...] = acc_s[...] / jnp.where(l > 0.0, l, 1.0)


def dec_attn(pt, q, cache, new, *, mode, extras=(), n_new, rows_per_t, t_period, window=NSA_WINDOW):
    B, n_pages = pt.shape
    _, R, Dq = q.shape
    W = cache.shape[-1]
    G = _pick_tile(n_pages, 8)
    nc = n_pages // G
    dv = 256 if mode == "mla" else 128
    if mode == "sb":
        page_idx = lambda b, c, pt, p: (pt[b, n_pages - 1 - (c * G + p)], 0, 0)
    else:
        page_idx = lambda b, c, pt, p: (pt[b, c * G + p], 0, 0)
    per_b = lambda shape: pl.BlockSpec((None,) + shape, lambda b, c, pt: (b,) + (0,) * len(shape))
    in_specs = [per_b((R, Dq))]
    if mode == "fox":
        in_specs += [per_b((R, LANES)), pl.BlockSpec((None, R, G * PAGE), lambda b, c, pt: (b, 0, c)), per_b((R, NEW_PAD))]
    if mode == "sel":
        in_specs += [pl.BlockSpec((None, None, R, 2 * G), lambda b, c, pt: (b, c, 0, 0)),
                     pl.BlockSpec((2 * G, G * PAGE), lambda b, c, pt: (0, 0)), per_b((R, NEW_PAD))]
    if mode == "sb":
        in_specs += [pl.BlockSpec((PAGE, PAGE), lambda b, c, pt: (0, 0))]
    in_specs += [pl.BlockSpec((None, PAGE, W), functools.partial(page_idx, p=p)) for p in range(G)]
    in_specs += [per_b((NEW_PAD, W))]
    return pl.pallas_call(
        functools.partial(_dec_body, mode=mode, G=G, nc=nc, n_new=n_new, rows_per_t=rows_per_t, t_period=t_period,
                          wlen=n_pages * PAGE, window=window),
        grid_spec=pltpu.PrefetchScalarGridSpec(
            num_scalar_prefetch=1,
            grid=(B, nc),
            in_specs=in_specs,
            out_specs=per_b((R, dv)),
            scratch_shapes=[pltpu.VMEM((R, LANES), F32), pltpu.VMEM((R, LANES), F32), pltpu.VMEM((R, dv), F32)],
        ),
        out_shape=jax.ShapeDtypeStruct((B, R, dv), F32),
        compiler_params=_cparams(("parallel", "arbitrary")),
        name="dec_" + mode,
    )(pt, q, *extras, *([cache] * G), new)


def _page_rows_body(pt_ref, *refs, G):
    del pt_ref
    o_ref = refs[G]
    o_ref[...] = jnp.concatenate([r[...] for r in refs[:G]], axis=0)


def gather_page_rows(pt, cache_rows):
    B, n_pages = pt.shape
    W = cache_rows.shape[-1]
    G = _pick_tile(n_pages, 8)
    return pl.pallas_call(
        functools.partial(_page_rows_body, G=G),
        grid_spec=pltpu.PrefetchScalarGridSpec(
            num_scalar_prefetch=1,
            grid=(B, n_pages // G),
            in_specs=[pl.BlockSpec((None, 1, W), lambda b, c, pt, p=p: (pt[b, c * G + p], 0, 0)) for p in range(G)],
            out_specs=pl.BlockSpec((None, G, W), lambda b, c, pt: (b, c, 0)),
        ),
        out_shape=jax.ShapeDtypeStruct((B, n_pages, W), cache_rows.dtype),
        compiler_params=_cparams(("parallel", "parallel")),
        name="gather_page_rows",
    )(pt, *([cache_rows] * G))


def _cmp_pages_body(pt_ref, *refs, G, nc):
    del pt_ref
    pages = refs[:G]
    w_ref, b_ref, o_ref, x_s = refs[G:]
    c = pl.program_id(1)
    rows = jnp.concatenate([p[...] for p in pages], axis=0).astype(BF)
    x_s[pl.ds(pl.multiple_of(c * (2 * G), 2 * G), 2 * G), :] = rows

    @pl.when(c == nc - 1)
    def _():
        o_ref[...] = _dot(x_s[...], w_ref[...]) + b_ref[...]


def compress_pages(pt, cache_blocks, wbig, bias):
    B, n_pages = pt.shape
    W = cache_blocks.shape[-1]
    G = _pick_tile(n_pages, 8)
    nc = n_pages // G
    return pl.pallas_call(
        functools.partial(_cmp_pages_body, G=G, nc=nc),
        grid_spec=pltpu.PrefetchScalarGridSpec(
            num_scalar_prefetch=1,
            grid=(B, nc),
            in_specs=[pl.BlockSpec((None, 2, W), lambda b, c, pt, p=p: (pt[b, c * G + p], 0, 0)) for p in range(G)]
            + [pl.BlockSpec(wbig.shape, lambda b, c, pt: (0, 0)), pl.BlockSpec(bias.shape, lambda b, c, pt: (0, 0))],
            out_specs=pl.BlockSpec((None, 2 * n_pages, 256), lambda b, c, pt: (b, 0, 0)),
            scratch_shapes=[pltpu.VMEM((2 * n_pages, W), BF)],
        ),
        out_shape=jax.ShapeDtypeStruct((B, 2 * n_pages, 256), F32),
        compiler_params=_cparams(("parallel", "arbitrary")),
        name="compress_pages",
    )(pt, *([cache_blocks] * G), wbig, bias)


def _page_prefix_body(within_ref, tot_ref, l_ref, o_ref):
    a, b, c = _split3(tot_ref[...])
    l = l_ref[...]
    o_ref[...] = within_ref[...] + _dot(l, a) + _dot(l, b) + _dot(l, c)


def page_prefix(y, lstrict):
    B, npg, W2 = y.shape
    W = W2 // 2
    return pl.pallas_call(
        _page_prefix_body,
        grid=(B,),
        in_specs=[pl.BlockSpec((None, npg, W), lambda b: (b, 0, 0)), pl.BlockSpec((None, npg, W), lambda b: (b, 0, 1)),
                  pl.BlockSpec((npg, npg), lambda b: (0, 0))],
        out_specs=pl.BlockSpec((None, npg, W), lambda b: (b, 0, 0)),
        out_shape=jax.ShapeDtypeStruct((B, npg, W), F32),
        compiler_params=_cparams(("parallel",)),
        name="page_prefix",
    )(y, y, lstrict)


def _swap_halves(w, hd):
    K, N = w.shape
    return w.reshape(K, N // hd, 2, hd // 2)[:, :, ::-1, :].reshape(K, N)


def _pad_last(a, n):
    return jnp.pad(a, [(0, 0)] * (a.ndim - 1) + [(0, n - a.shape[-1])])


def _rope_tabs(pos, d):
    inv = ROPE_THETA ** (-jnp.arange(0, d, 2, dtype=F32) / d)
    ang = pos.astype(F32)[:, None] * inv[None, :]
    c, s = jnp.cos(ang), jnp.sin(ang)
    return jnp.concatenate([c, c], axis=1), jnp.concatenate([-s, s], axis=1)


def _prep_weights(W):
    D = W["ev_w_in"].shape[1]
    P = {}
    w = W["ev_w_in"][0]
    kr = w[:, 1280:1312]
    P["ev_wext"] = jnp.concatenate([w[:, :1280], _pad_last(kr, 128), _pad_last(_swap_halves(kr, 32), 128)], axis=1).astype(BF)
    wuq = W["ev_w_uq"][0]
    wn, wr = wuq[:, :, :MLA_NOPE].reshape(256, 512), wuq[:, :, MLA_NOPE:].reshape(256, 256)
    P["ev_wuq"] = jnp.concatenate([wn, wr, _swap_halves(wr, MLA_ROPE)], axis=1).astype(BF)
    wuk, wuv = W["ev_w_uk"][0], W["ev_w_uv"][0]
    eye8 = jnp.eye(MLA_HEADS, dtype=F32)
    P["ev_wuk"] = wuk.reshape(256, 512).astype(BF)
    P["ev_wuv"] = wuv.reshape(256, 512).astype(BF)
    P["ev_wabs"] = jnp.einsum("chn,hH->hnHc", wuk, eye8).reshape(512, 2048).astype(BF)
    P["ev_wuv_bd"] = jnp.einsum("chv,hH->hcHv", wuv, eye8).reshape(2048, 512).astype(BF)
    P["ev_wout"] = W["ev_w_out"][0].astype(BF)
    P["ev_gq"] = W["ev_q_norm"][0][None, :]
    P["ev_gkv"] = W["ev_kv_norm"][0][None, :]

    w = W["od_w_in"][0]
    seg = lambda a, b: w[:, a:b]
    nq, ck, cv, sk, sv, wk, wv = seg(0, 512), seg(512, 640), seg(640, 768), seg(768, 896), seg(896, 1024), seg(1024, 1152), seg(1152, 1280)
    gl, fq, fk, fv, fl = seg(1280, 1304), seg(1304, 1816), seg(1816, 1944), seg(1944, 2072), seg(2072, 2080)
    sw = lambda a: _swap_halves(a, HEAD_DIM)
    P["od_wext"] = jnp.concatenate([nq, sw(nq), ck, sw(ck), cv, sk, sw(sk), sv, wk, sw(wk), wv, _pad_last(gl, 128), fq, fk, fv,
                                    _pad_last(fl, 128)], axis=1).astype(BF)
    P["od_fb"] = _pad_last(W["od_fox_fb"][0][None, :], 128)
    P["od_wout"] = W["od_w_out"][0].astype(BF)
    eye2 = jnp.eye(2, dtype=F32)
    wbig = jnp.einsum("klde,kK,gG->lkgdKGe", W["od_cmp_w"][0], eye2, eye2).reshape(NSA_BLOCK * 256, 256)
    pe_flat = jnp.broadcast_to(W["od_cmp_pe"][0].transpose(1, 0, 2)[:, :, None, :], (NSA_BLOCK, 2, 2, HEAD_DIM)).reshape(1, -1)
    P["cmp_w"] = wbig.astype(BF)
    P["cmp_b"] = linear(jnp.pad(pe_flat, ((0, 7), (0, 0))), wbig, mode="hi", tn=128, name="cmp_bias")[:1]
    r = np.arange(128)
    cidx = np.arange(1536)
    P["gate_expand"] = jnp.asarray(((r[:, None] // 3 == (cidx[None, :] % 512) // 64) & (r[:, None] % 3 == cidx[None, :] // 512)
                                    & (r[:, None] < 24)).astype(np.float32), dtype=BF)
    i = np.arange(PAGE * 8)
    same_h = (i[:, None] % 8) == (i[None, :] % 8)
    P["scan_w"] = jnp.asarray(np.concatenate([same_h & (i[:, None] // 8 <= i[None, :] // 8), same_h], axis=1).astype(np.float32),
                              dtype=BF)
    P["moe_wg"] = W["moe_w_gate"].astype(BF)
    P["moe_wu"] = W["moe_w_up"].astype(BF)
    P["moe_wd"] = W["moe_w_down"].astype(BF)
    return P


def _mods(c, W, l, T, per_token):
    B, D = c.shape
    m = linear(c, W["ada_w"], W["ada_b"].reshape(W["ada_b"].shape[0], 1, -1), widx=l, act="silu", name="ada_mod").reshape(B, 6, D)
    if per_token:
        return [jnp.repeat(m[:, j], T, axis=0)[None] for j in range(6)]
    return [m[:, j][:, None, :] for j in range(6)]


def _heads_major(a, B, T, G, hg):
    return a.reshape(B, T, G, hg, HEAD_DIM).transpose(0, 2, 3, 1, 4)


def _kv_major(a, B, T, G):
    return a.reshape(B, T, G, HEAD_DIM).transpose(0, 2, 1, 3)


def _from_heads_major(o, N):
    B, G, hg, T, d = o.shape
    return o.transpose(0, 3, 1, 2, 4).reshape(N, G * hg * d)


def _block_diag_q(q, B, T):
    q5 = q.reshape(B, T, 2, 4, HEAD_DIM).transpose(0, 2, 1, 3, 4).reshape(B, 2, T * 4, 1, HEAD_DIM)
    eye = jnp.eye(2, dtype=q.dtype)[None, :, None, :, None]
    return (q5 * eye).reshape(B, 2 * T * 4, 2 * HEAD_DIM)


def _pick_diag(o, B, T):
    o6 = o.reshape(B, 2, T, 4, 2, HEAD_DIM)
    return jnp.stack([o6[:, 0, :, :, 0], o6[:, 1, :, :, 1]], axis=2).reshape(B * T, 512)


def _pad_new(rows, B, T):
    return jnp.pad(rows.reshape(B, T, -1), ((0, 0), (0, NEW_PAD - T), (0, 0)))


def _head_rows(T):
    kv, t, h = np.meshgrid(np.arange(2), np.arange(T), np.arange(4), indexing="ij")
    return (kv * 4 + h).reshape(-1), t.reshape(-1)


def _scan_logf(x_pages, P):
    B, npg, Wd = x_pages.shape
    y = linear(x_pages.reshape(B * npg, Wd), P["scan_w"], mode="split3", name="scan_within")
    lstrict = jnp.asarray(np.tril(np.ones((npg, npg), np.float32), -1), dtype=BF)
    return page_prefix(y.reshape(B, npg, 2 * Wd), lstrict).reshape(B, npg * PAGE, 8)


def _trunk(x, c, W, P, past):
    B, T, D = x.shape
    N = B * T
    sample = past is not None
    x2 = x.reshape(N, D)
    tm = N if sample else _pick_tile(T, 256)
    mod_rows = N if sample else T
    kw = dict(tm=tm, mod_rows=mod_rows)
    pos0 = past["len"] if sample else 0
    pos = pos0 + jnp.arange(T, dtype=jnp.int32)
    rep = (lambda a: jnp.tile(a, (B, 1))) if sample else (lambda a: a)
    tab_rows = N if sample else T
    out = {}

    sh_a, sc_a, gt_a, sh_m, sc_m, gt_m = _mods(c, W, 0, T, sample)
    c32, s32 = _rope_tabs(pos, MLA_ROPE)
    tabs = [rep(_pad_last(c32, 128)), rep(_pad_last(s32, 128)), rep(jnp.tile(c32, (1, 8))), rep(jnp.tile(s32, (1, 8)))]
    g0 = W["norm_mix"][0][None, :]
    if not sample:
        sbq, sbrows, mla, qn, qr, kn, vv = even_front(x2, g0, sc_a, sh_a, P["ev_wext"], tabs, P["ev_gq"], P["ev_gkv"], P["ev_wuq"],
                                                       P["ev_wuk"], P["ev_wuv"], absorbed=False, tab_rows=tab_rows, **kw)
        t_att = _pick_tile(T, 256)
        u = jnp.asarray(np.tril(np.ones((t_att, t_att), np.float32), -1), dtype=BF)
        sbb = sbrows.astype(BF)
        o_sb = pflash(_heads_major(sbq, B, T, 2, 4), _kv_major(sbb[:, :128], B, T, 2), _kv_major(sbb[:, 128:], B, T, 2),
                      mode="sb", t=t_att, extras=(u,))
        krb = jnp.broadcast_to(mla[:, 256:].astype(BF).reshape(B, T, 1, MLA_ROPE), (B, T, MLA_HEADS, MLA_ROPE))
        qm = jnp.concatenate([qn.reshape(B, T, 8, 64), qr.reshape(B, T, 8, 32)], axis=-1).transpose(0, 2, 1, 3)[:, :, None]
        km = jnp.concatenate([kn.reshape(B, T, 8, 64), krb], axis=-1).transpose(0, 2, 1, 3)
        vm = vv.reshape(B, T, 8, 64).transpose(0, 2, 1, 3)
        o_mla = pflash(qm, km, vm, mode="causal", t=_pick_tile(T, 512))
        mix = jnp.concatenate([_from_heads_major(o_sb, N), _from_heads_major(o_mla, N)], axis=1)
    else:
        pt = past["pt"]
        sbq, sbrows, mla, qlat, qr = even_front(x2, g0, sc_a, sh_a, P["ev_wext"], tabs, P["ev_gq"], P["ev_gkv"], P["ev_wuq"],
                                                P["ev_wabs"], None, absorbed=True, tab_rows=tab_rows, **kw)
        u = jnp.asarray(np.tril(np.ones((PAGE, PAGE), np.float32), -1), dtype=BF)
        n_pool = past["sb"].shape[0]
        o = dec_attn(pt, _block_diag_q(sbq, B, T), past["sb"].reshape(n_pool, PAGE, 256), _pad_new(sbrows, B, T), mode="sb",
                     extras=(u,), n_new=T, rows_per_t=4, t_period=4 * T)
        o_sb = _pick_diag(o, B, T).astype(BF)
        qcat = jnp.concatenate([qlat.reshape(B, T, 8, 256), qr.reshape(B, T, 8, 32)], axis=-1).reshape(B, T * 8, 288)
        o_lat = dec_attn(pt, qcat, past["mla"].reshape(n_pool, PAGE, 288), _pad_new(mla, B, T), mode="mla", n_new=T,
                         rows_per_t=8, t_period=8 * T)
        o_mla = linear(o_lat.reshape(N, 2048), P["ev_wuv_bd"], out_dtype=BF, name="mla_v_up")
        mix = jnp.concatenate([o_sb, o_mla], axis=1)
    out["sb"] = sbrows.reshape(1, B, T, 2, SB_KV_HEADS, HEAD_DIM)
    out["mla"] = mla.reshape(1, B, T, 288)
    x2 = out_proj_res(mix, P["ev_wout"], x2, gt_a, **kw)
    x2 = moe_layer(x2, W["norm_ffn"][0][None, :], sc_m, sh_m, gt_m, P["wgrp"][0], P["bgrp"][0], P["wrt"][0], P["brt"][0],
                   P["moe_wg"], P["moe_wu"], P["moe_wd"], 0, **kw)

    sh_a, sc_a, gt_a, sh_m, sc_m, gt_m = _mods(c, W, 1, T, sample)
    c64, s64 = _rope_tabs(pos, HEAD_DIM)
    tabs = [rep(jnp.tile(c64, (1, 8))), rep(jnp.tile(s64, (1, 8))), rep(jnp.tile(c64, (1, 2))), rep(jnp.tile(s64, (1, 2)))]
    nq, nqf, cmp_rows, slc_rows, win_rows, gates, fq, fox_rows, logf = odd_front(
        x2, W["norm_mix"][1][None, :], sc_a, sh_a, P["od_wext"], tabs, P["od_fb"], tab_rows=tab_rows, **kw)
    logf8 = logf[:, :8]
    G, hg = NSA_GROUPS, NSA_HEADS // NSA_GROUPS
    if not sample:
        nb = T // NSA_BLOCK
        nbp = _round_up(nb, LANES)
        kcv = linear(cmp_rows.astype(BF).reshape(B * nb, NSA_BLOCK * 256), P["cmp_w"], P["cmp_b"], tm=128, tn=256, name="compress")
        kcv = kcv.reshape(B, nb, 2, G, HEAD_DIM).transpose(2, 0, 3, 1, 4)
        kcv = jnp.pad(kcv, ((0, 0), (0, 0), (0, 0), (0, nbp - nb), (0, 0)))
        tq = _pick_tile(T, 128)
        nqt = T // tq
        qf = nqf.reshape(B, nqt, tq, G, hg, HEAD_DIM).transpose(0, 3, 1, 4, 2, 5).reshape(B, G, nqt, hg * tq, HEAD_DIM)
        oc, sel = nsa_cmp_sel(qf, kcv[0], kcv[1], tq=tq, nbc=nb, nbs=nb, pos0=0)
        oc = oc.reshape(B, G, nqt, hg, tq, HEAD_DIM).transpose(0, 2, 4, 1, 3, 5).reshape(N, 512)
        sel = sel.reshape(B, G, T, nbp)
        expand = jnp.asarray((np.arange(nbp)[:, None] == np.arange(T)[None, :] // NSA_BLOCK).astype(np.float32), dtype=BF)
        t_att = _pick_tile(T, 256)
        q5 = _heads_major(nq, B, T, G, hg)
        slb, wnb, fxb = slc_rows.astype(BF), win_rows.astype(BF), fox_rows.astype(BF)
        osel = pflash(q5, _kv_major(slb[:, :128], B, T, G), _kv_major(slb[:, 128:], B, T, G), mode="sel", t=t_att, extras=(sel, expand))
        ow = pflash(q5, _kv_major(wnb[:, :128], B, T, G), _kv_major(wnb[:, 128:], B, T, G), mode="win", t=t_att)
        cs = _scan_logf(logf8.reshape(B, T // PAGE, PAGE * 8), P)
        ch = cs.transpose(0, 2, 1).reshape(B, 2, 4, T)
        cq = jnp.broadcast_to(ch[..., None], (B, 2, 4, T, LANES))
        of = pflash(_heads_major(fq, B, T, 2, 4), _kv_major(fxb[:, :128], B, T, 2), _kv_major(fxb[:, 128:], B, T, 2), mode="fox",
                    t=t_att, extras=(cq, ch[:, :, :, None, :]))
        osel, ow, of = _from_heads_major(osel, N), _from_heads_major(ow, N), _from_heads_major(of, N)
        wb = min(NSA_WINDOW, T)
        out["win"] = win_rows.reshape(1, B, T, 2, G, HEAD_DIM)[:, :, T - wb:]
    else:
        n_pages = pt.shape[1]
        L = n_pages * PAGE
        nbc = L // NSA_BLOCK
        nbs = nbc + 1
        nbp = _round_up(nbs, LANES)
        kcv = compress_pages(pt, past["cmp"].reshape(n_pool, 2, NSA_BLOCK * 256), P["cmp_w"], P["cmp_b"])
        kcv = kcv.reshape(B, nbc, 2, G, HEAD_DIM).transpose(2, 0, 3, 1, 4)
        kcv = jnp.pad(kcv, ((0, 0), (0, 0), (0, 0), (0, nbp - nbc), (0, 0)))
        qf = nqf.reshape(B, T, G, hg, HEAD_DIM).transpose(0, 2, 3, 1, 4).reshape(B, G, 1, hg * T, HEAD_DIM)
        oc, sel = nsa_cmp_sel(qf, kcv[0], kcv[1], tq=T, nbc=nbc, nbs=nbs, pos0=pos0)
        oc = oc.reshape(B, G, hg, T, HEAD_DIM).transpose(0, 3, 1, 2, 4).reshape(N, 512)
        sel = sel.reshape(B, G, T, nbp)
        Gp = _pick_tile(n_pages, 8)
        nc = n_pages // Gp
        selr = jnp.broadcast_to(sel[:, :, :, None, :], (B, G, T, hg, nbp)).reshape(B, G * T * hg, nbp)
        selx = selr[:, :, :2 * n_pages].reshape(B, G * T * hg, nc, 2 * Gp).transpose(0, 2, 1, 3)
        selnew = jnp.broadcast_to(selr[:, :, 2 * n_pages:2 * n_pages + 1].astype(F32), (B, G * T * hg, NEW_PAD))
        e2 = jnp.asarray((np.arange(2 * Gp)[:, None] == np.arange(Gp * PAGE)[None, :] // NSA_BLOCK).astype(np.float32), dtype=BF)
        rk = dict(n_new=T, rows_per_t=4, t_period=4 * T)
        qbd = _block_diag_q(nq, B, T)
        osel = dec_attn(pt, qbd, past["slc"].reshape(n_pool, PAGE, 256), _pad_new(slc_rows, B, T), mode="sel",
                        extras=(selx, e2, selnew), **rk)
        win_buf = past["win"]
        wlen = win_buf.shape[1]
        npw = wlen // PAGE
        pt_win = jnp.arange(B * npw, dtype=jnp.int32).reshape(B, npw)
        ow = dec_attn(pt_win, qbd, win_buf.reshape(B * npw, PAGE, 256), _pad_new(win_rows, B, T), mode="win", **rk)
        cpast = _scan_logf(gather_page_rows(pt, past["logf"].reshape(n_pool, 1, PAGE * 8)), P)
        lnew = logf8.reshape(B, T, 8)
        run = cpast[:, -1, :]
        cnew = []
        for tt in range(T):
            run = run + lnew[:, tt, :]
            cnew.append(run)
        cnew = jnp.stack(cnew, axis=1)
        hrow, trow = _head_rows(T)
        ck32 = cpast.transpose(0, 2, 1)[:, hrow, :]
        cq32 = jnp.broadcast_to(cnew[:, trow, hrow][:, :, None], (B, len(hrow), LANES))
        cknew = _pad_last(cnew.transpose(0, 2, 1)[:, hrow, :], NEW_PAD)
        of = dec_attn(pt, _block_diag_q(fq, B, T), past["fox"].reshape(n_pool, PAGE, 256), _pad_new(fox_rows, B, T), mode="fox",
                      extras=(cq32, ck32, cknew), **rk)
        osel, ow, of = (_pick_diag(a, B, T).astype(BF) for a in (osel, ow, of))
        out["win"] = jnp.concatenate([win_buf, win_rows.reshape(B, T, 2, G, HEAD_DIM)], axis=1)[None, :, T:]
    mix = nsa_combine(oc, osel, ow, gates, P["gate_expand"], of, tm=tm)
    out["cmp"] = cmp_rows.reshape(1, B, T, 2, G, HEAD_DIM)
    out["slc"] = slc_rows.reshape(1, B, T, 2, G, HEAD_DIM)
    out["fox"] = fox_rows.reshape(1, B, T, 2, FOX_KV_HEADS, HEAD_DIM)
    out["logf"] = logf8.reshape(1, B, T, FOX_HEADS)
    x2 = out_proj_res(mix, P["od_wout"], x2, gt_a, **kw)
    x2 = moe_layer(x2, W["norm_ffn"][1][None, :], sc_m, sh_m, gt_m, P["wgrp"][1], P["bgrp"][1], P["wrt"][1], P["brt"][1],
                   P["moe_wg"], P["moe_wu"], P["moe_wd"], 1, **kw)
    y = final_norm(x2, W["norm_final"][None, :], tm=tm).reshape(B, T, D)
    return y, out


def kernel(x_prompt, x_sample, c_prompt, c_sample, cache_sb_kv, cache_mla_latent, cache_nsa_cmp_kv, cache_nsa_slc_kv,
           cache_fox_kv, cache_fox_logf, state_nsa_win_kv, page_table, ada_w, ada_b, norm_mix, norm_ffn, norm_final,
           ev_w_in, ev_q_norm, ev_kv_norm, ev_w_uq, ev_w_uk, ev_w_uv, ev_w_out, od_w_in, od_cmp_pe, od_cmp_w, od_fox_fb,
           od_w_out, moe_w_grp, moe_b_grp, moe_w_rt, moe_b_rt, moe_w_gate, moe_w_up, moe_w_down):
    assert ada_w.shape[0] == 2, "two trunk layers (one even, one odd) are implemented"
    W = dict(ada_w=ada_w, ada_b=ada_b, norm_mix=norm_mix, norm_ffn=norm_ffn, norm_final=norm_final, ev_w_in=ev_w_in,
             ev_q_norm=ev_q_norm, ev_kv_norm=ev_kv_norm, ev_w_uq=ev_w_uq, ev_w_uk=ev_w_uk, ev_w_uv=ev_w_uv, ev_w_out=ev_w_out,
             od_w_in=od_w_in, od_cmp_pe=od_cmp_pe, od_cmp_w=od_cmp_w, od_fox_fb=od_fox_fb, od_w_out=od_w_out,
             moe_w_gate=moe_w_gate, moe_w_up=moe_w_up, moe_w_down=moe_w_down)
    P = _prep_weights(W)
    P["wgrp"] = _pad_last(moe_w_grp, LANES)
    P["bgrp"] = _pad_last(moe_b_grp, LANES)[:, None, :]
    P["wrt"] = _pad_last(moe_w_rt, LANES)
    P["brt"] = _pad_last(moe_b_rt, LANES)[:, None, :]
    past = dict(pt=page_table, len=page_table.shape[1] * cache_sb_kv.shape[2], sb=cache_sb_kv[0], mla=cache_mla_latent[0],
                cmp=cache_nsa_cmp_kv[0], slc=cache_nsa_slc_kv[0], fox=cache_fox_kv[0], logf=cache_fox_logf[0],
                win=state_nsa_win_kv[0])
    y_p, sp = _trunk(x_prompt, c_prompt, W, P, None)
    y_s, ss = _trunk(x_sample, c_sample, W, P, past)
    names = ("sb", "mla", "cmp", "slc", "fox", "logf", "win")
    outs = [y_p, y_s]
    for n in names:
        outs += [sp[n], ss[n]]
    return tuple(outs)
```

```python
import functools
import math

import jax
import jax.numpy as jnp
import numpy as np
from jax import lax
from jax.experimental import pallas as pl
from jax.experimental.pallas import tpu as pltpu

F32 = jnp.float32
BF = jnp.bfloat16
HI = lax.Precision.HIGHEST

HEAD_DIM = 64
ROPE_THETA = 10000.0
RMS_EPS = 1e-6
SB_HEADS, SB_KV_HEADS = 8, 2
MLA_HEADS, MLA_NOPE, MLA_ROPE, MLA_V = 8, 64, 32, 64
NSA_HEADS, NSA_GROUPS, NSA_BLOCK, NSA_TOPN, NSA_WINDOW = 8, 2, 64, 16, 512
FOX_HEADS, FOX_KV_HEADS = 8, 2
MOE_GROUPS, MOE_PER_GROUP, MOE_TOPK = 4, 8, 2
MOE_EXPERTS = MOE_GROUPS * MOE_PER_GROUP
MLA_SCALE = 1.0 / math.sqrt(MLA_NOPE + MLA_ROPE)
QK_SCALE = 1.0 / math.sqrt(HEAD_DIM)
LANES = 128
VMEM_LIMIT_MB = 56
ATT_TILES = dict(tq=256, tk=256, gb=2)
MLA_TILES = dict(tq=512, tk=512, gb=4)
MOE_TOKEN_TILE = 1024
MOE_EXPERTS_PER_STEP = 4


def _cparams(sem, vmem_mb=VMEM_LIMIT_MB):
    return pltpu.CompilerParams(dimension_semantics=sem, vmem_limit_bytes=vmem_mb * 2 ** 20)


def _round_up(a, m):
    return (a + m - 1) // m * m


def _pick_tile(n, pref):
    t = min(pref, n)
    while n % t:
        t //= 2
    return t


def _rms(x, g):
    return x * lax.rsqrt(jnp.mean(x * x, axis=-1, keepdims=True) + RMS_EPS) * g


def _dot(a, b):
    return jnp.dot(a, b, preferred_element_type=F32)


def _dot_t(a, b, precision=None):
    return lax.dot_general(a, b, (((1,), (1,)), ((), ())), preferred_element_type=F32, precision=precision)


def _split3(x):
    a = x.astype(BF)
    r = x - a.astype(F32)
    b = r.astype(BF)
    c = (r - b.astype(F32)).astype(BF)
    return a, b, c


def _linear_body(*refs, act, mode, has_bias):
    x_ref, w_ref = refs[0], refs[1]
    b_ref = refs[2] if has_bias else None
    o_ref = refs[-1]
    x = x_ref[...]
    if act == "silu":
        x = x.astype(F32)
        x = x * jax.nn.sigmoid(x)
    w = w_ref[...]
    if mode == "hi":
        y = jnp.dot(x.astype(F32), w.astype(F32), precision=HI, preferred_element_type=F32)
    elif mode == "split3":
        a, b, c = _split3(x.astype(F32))
        wb = w.astype(BF)
        y = _dot(a, wb) + _dot(b, wb) + _dot(c, wb)
    else:
        y = _dot(x.astype(BF), w.astype(BF))
    if has_bias:
        y = y + b_ref[...]
    o_ref[...] = y.astype(o_ref.dtype)


def linear(x, w, bias=None, *, widx=None, act=None, mode="bf16", out_dtype=F32, tm=256, tn=512, name="linear"):
    M, K = x.shape
    N = w.shape[-1]
    tm = _pick_tile(M, tm)
    tn = _pick_tile(N, tn)
    if w.ndim == 3:
        w_spec = pl.BlockSpec((None, K, tn), lambda i, j: (widx, 0, j))
    else:
        w_spec = pl.BlockSpec((K, tn), lambda i, j: (0, j))
    in_specs = [pl.BlockSpec((tm, K), lambda i, j: (i, 0)), w_spec]
    args = [x, w]
    if bias is not None:
        if bias.ndim == 3:
            in_specs.append(pl.BlockSpec((None, 1, tn), lambda i, j: (widx, 0, j)))
        else:
            in_specs.append(pl.BlockSpec((1, tn), lambda i, j: (0, j)))
        args.append(bias)
    return pl.pallas_call(
        functools.partial(_linear_body, act=act, mode=mode, has_bias=bias is not None),
        grid=(M // tm, N // tn),
        in_specs=in_specs,
        out_specs=pl.BlockSpec((tm, tn), lambda i, j: (i, j)),
        out_shape=jax.ShapeDtypeStruct((M, N), out_dtype),
        compiler_params=_cparams(("parallel", "parallel")),
        name=name,
    )(*args)


def _mod_spec(tm, D, mod_rows):
    return lambda R: pl.BlockSpec((None, R, D), lambda i: ((i * tm) // mod_rows, 0, 0))


def _even_front_body(x_ref, g_ref, sc_ref, sh_ref, w_ref, c32_ref, s32_ref, c256_ref, s256_ref, gq_ref, gkv_ref,
                     wuq_ref, wa_ref, wb_ref, sbq_ref, sbrows_ref, mla_ref, o1_ref, o2_ref, o3_ref, o4_ref, *, absorbed):
    hn = _rms(x_ref[...], g_ref[...]) * (1.0 + sc_ref[...]) + sh_ref[...]
    proj = _dot(hn.astype(BF), w_ref[...])
    sbq_ref[...] = (proj[:, :512] * QK_SCALE).astype(BF)
    sbrows_ref[...] = proj[:, 512:768]
    cqn = _rms(proj[:, 768:1024], gq_ref[...])
    ckvn = _rms(proj[:, 1024:1280], gkv_ref[...])
    krr = proj[:, 1280:1408] * c32_ref[...] + proj[:, 1408:1536] * s32_ref[...]
    mla_ref[:, :256] = ckvn
    mla_ref[:, 256:288] = krr[:, :32]
    qu = _dot(cqn.astype(BF), wuq_ref[...])
    qr = (qu[:, 512:768] * c256_ref[...] + qu[:, 768:1024] * s256_ref[...]) * MLA_SCALE
    qn = qu[:, :512]
    o2_ref[...] = qr.astype(BF)
    if absorbed:
        o1_ref[...] = (_dot(qn.astype(BF), wa_ref[...]) * MLA_SCALE).astype(BF)
    else:
        o1_ref[...] = (qn * MLA_SCALE).astype(BF)
        ckb = ckvn.astype(BF)
        o3_ref[...] = _dot(ckb, wa_ref[...]).astype(BF)
        o4_ref[...] = _dot(ckb, wb_ref[...]).astype(BF)


def even_front(x2, g, sc, sh, wext, tabs, gq, gkv, wuq, wa, wb, *, tm, mod_rows, tab_rows, absorbed):
    N, D = x2.shape
    R = sc.shape[1]
    c32, s32, c256, s256 = tabs
    nt = tab_rows // tm
    row = lambda w: pl.BlockSpec((tm, w), lambda i: (i, 0))
    tab = lambda w: pl.BlockSpec((tm, w), lambda i: (i % nt, 0))
    full = lambda a: pl.BlockSpec(a.shape, lambda i: (0,) * a.ndim)
    mod = _mod_spec(tm, D, mod_rows)(R)
    in_specs = [row(D), full(g), mod, mod, full(wext), tab(128), tab(128), tab(256), tab(256), full(gq), full(gkv),
                full(wuq), full(wa)]
    args = [x2, g, sc, sh, wext, c32, s32, c256, s256, gq, gkv, wuq, wa]
    if absorbed:
        body = lambda *r: _even_front_body(*r[:13], None, *r[13:], None, None, absorbed=True)
        outs = [(512, BF), (256, F32), (288, F32), (2048, BF), (256, BF)]
    else:
        in_specs.append(full(wb))
        args.append(wb)
        body = functools.partial(_even_front_body, absorbed=False)
        outs = [(512, BF), (256, F32), (288, F32), (512, BF), (256, BF), (512, BF), (512, BF)]
    return pl.pallas_call(
        body,
        grid=(N // tm,),
        in_specs=in_specs,
        out_specs=[row(w) for w, _ in outs],
        out_shape=[jax.ShapeDtypeStruct((N, w), dt) for w, dt in outs],
        compiler_params=_cparams(("parallel",)),
        name="even_front",
    )(*args)


def _log_sigmoid(z):
    return jnp.minimum(z, 0.0) - jnp.log(1.0 + jnp.exp(-jnp.abs(z)))


def _odd_front_body(x_ref, g_ref, sc_ref, sh_ref, w_ref, c512_ref, s512_ref, c128_ref, s128_ref, fb_ref,
                    nq_ref, nqf_ref, cmp_ref, slc_ref, win_ref, gates_ref, fq_ref, fox_ref, logf_ref):
    hn = _rms(x_ref[...], g_ref[...]) * (1.0 + sc_ref[...]) + sh_ref[...]
    p = _dot(hn.astype(BF), w_ref[...])
    nq = (p[:, 0:512] * c512_ref[...] + p[:, 512:1024] * s512_ref[...]) * QK_SCALE
    nq_ref[...] = nq.astype(BF)
    nqf_ref[...] = nq
    c128, s128 = c128_ref[...], s128_ref[...]
    cmp_ref[:, :128] = p[:, 1024:1152] * c128 + p[:, 1152:1280] * s128
    cmp_ref[:, 128:] = p[:, 1280:1408]
    slc_ref[:, :128] = p[:, 1408:1536] * c128 + p[:, 1536:1664] * s128
    slc_ref[:, 128:] = p[:, 1664:1792]
    win_ref[:, :128] = p[:, 1792:1920] * c128 + p[:, 1920:2048] * s128
    win_ref[:, 128:] = p[:, 2048:2176]
    gates_ref[...] = jax.nn.sigmoid(p[:, 2176:2304])
    fq_ref[...] = (p[:, 2304:2816] * QK_SCALE).astype(BF)
    fox_ref[...] = p[:, 2816:3072]
    logf_ref[...] = _log_sigmoid(p[:, 3072:3200] + fb_ref[...])


def odd_front(x2, g, sc, sh, wext, tabs, fb, *, tm, mod_rows, tab_rows):
    N, D = x2.shape
    R = sc.shape[1]
    nt = tab_rows // tm
    row = lambda w: pl.BlockSpec((tm, w), lambda i: (i, 0))
    tab = lambda w: pl.BlockSpec((tm, w), lambda i: (i % nt, 0))
    full = lambda a: pl.BlockSpec(a.shape, lambda i: (0,) * a.ndim)
    mod = _mod_spec(tm, D, mod_rows)(R)
    outs = [(512, BF), (512, F32), (256, F32), (256, F32), (256, F32), (128, F32), (512, BF), (256, F32), (128, F32)]
    return pl.pallas_call(
        _odd_front_body,
        grid=(N // tm,),
        in_specs=[row(D), full(g), mod, mod, full(wext), tab(512), tab(512), tab(128), tab(128), full(fb)],
        out_specs=[row(w) for w, _ in outs],
        out_shape=[jax.ShapeDtypeStruct((N, w), dt) for w, dt in outs],
        compiler_params=_cparams(("parallel",)),
        name="odd_front",
    )(x2, g, sc, sh, wext, *tabs, fb)


def _outproj_body(mix_ref, w_ref, x_ref, gt_ref, o_ref):
    o_ref[...] = x_ref[...] + gt_ref[...] * _dot(mix_ref[...].astype(BF), w_ref[...])


def out_proj_res(mix, w, x2, gt, *, tm, mod_rows):
    N, D = x2.shape
    K = mix.shape[1]
    return pl.pallas_call(
        _outproj_body,
        grid=(N // tm,),
        in_specs=[pl.BlockSpec((tm, K), lambda i: (i, 0)), pl.BlockSpec(w.shape, lambda i: (0, 0)),
                  pl.BlockSpec((tm, D), lambda i: (i, 0)), _mod_spec(tm, D, mod_rows)(gt.shape[1])],
        out_specs=pl.BlockSpec((tm, D), lambda i: (i, 0)),
        out_shape=jax.ShapeDtypeStruct((N, D), F32),
        compiler_params=_cparams(("parallel",)),
        name="out_proj_res",
    )(mix, w, x2, gt)


def _nsa_combine_body(oc_ref, os_ref, ow_ref, g_ref, e_ref, of_ref, o_ref):
    a, b, c = _split3(g_ref[...])
    e = e_ref[...]
    gx = _dot(a, e) + _dot(b, e) + _dot(c, e)
    o = (gx[:, :512] * oc_ref[...].astype(F32) + gx[:, 512:1024] * os_ref[...].astype(F32)
         + gx[:, 1024:] * ow_ref[...].astype(F32))
    o_ref[:, :512] = o.astype(BF)
    o_ref[:, 512:] = of_ref[...]


def nsa_combine(oc, osel, ow, gates, expand, of, *, tm):
    N = oc.shape[0]
    row = lambda w: pl.BlockSpec((tm, w), lambda i: (i, 0))
    return pl.pallas_call(
        _nsa_combine_body,
        grid=(N // tm,),
        in_specs=[row(512), row(512), row(512), row(128), pl.BlockSpec(expand.shape, lambda i: (0, 0)), row(512)],
        out_specs=row(1024),
        out_shape=jax.ShapeDtypeStruct((N, 1024), BF),
        compiler_params=_cparams(("parallel",)),
        name="nsa_combine",
    )(oc, osel, ow, gates, expand, of)


def _final_norm_body(x_ref, g_ref, o_ref):
    o_ref[...] = _rms(x_ref[...], g_ref[...])


def final_norm(x2, g, *, tm):
    N, D = x2.shape
    return pl.pallas_call(
        _final_norm_body,
        grid=(N // tm,),
        in_specs=[pl.BlockSpec((tm, D), lambda i: (i, 0)), pl.BlockSpec((1, D), lambda i: (0, 0))],
        out_specs=pl.BlockSpec((tm, D), lambda i: (i, 0)),
        out_shape=jax.ShapeDtypeStruct((N, D), F32),
        compiler_params=_cparams(("parallel",)),
        name="final_norm",
    )(x2, g)


def _moe_body(x_ref, g_ref, sc_ref, sh_ref, gt_ref, wr_ref, br_ref, wg_ref, wu_ref, wd_ref, o_ref, hn_s, gates_s, acc_s):
    e = pl.program_id(1)
    lane = lax.broadcasted_iota(jnp.int32, gates_s.shape, 1)
    lanef = lane.astype(F32)

    @pl.when(e == 0)
    def _():
        hn = _rms(x_ref[...], g_ref[...]) * (1.0 + sc_ref[...]) + sh_ref[...]
        hn_s[...] = hn.astype(BF)
        lr = jnp.dot(hn, wr_ref[...], precision=HI, preferred_element_type=F32) + br_ref[...]
        lg = jnp.where((lane >= MOE_EXPERTS) & (lane < MOE_EXPERTS + MOE_GROUPS), lr, -jnp.inf)
        mg = jnp.max(lg, axis=1, keepdims=True)
        g_sel = jnp.min(jnp.where(lg == mg, lanef, 1e9), axis=1, keepdims=True) - MOE_EXPERTS
        g_w = 1.0 / jnp.sum(jnp.exp(lg - mg), axis=1, keepdims=True)
        ingrp = (lane >> 3).astype(F32) == g_sel
        ingrp = ingrp & (lane < MOE_EXPERTS)
        lrm = jnp.where(ingrp, lr, -jnp.inf)
        m1 = jnp.max(lrm, axis=1, keepdims=True)
        ex = jnp.exp(lrm - m1)
        i1 = jnp.min(jnp.where(ingrp & (ex == 1.0), lanef, 1e9), axis=1, keepdims=True)
        c2 = jnp.where(ingrp & (lanef != i1), ex, -1.0)
        v2 = jnp.max(c2, axis=1, keepdims=True)
        i2 = jnp.min(jnp.where(c2 == v2, lanef, 1e9), axis=1, keepdims=True)
        den = 1.0 + v2
        gates_s[...] = jnp.where(lanef == i1, g_w / den, 0.0) + jnp.where(lanef == i2, g_w * v2 / den, 0.0)
        acc_s[...] = jnp.zeros_like(acc_s)

    hb = hn_s[...]
    gates = gates_s[...]
    hids = []
    for k in range(MOE_EXPERTS_PER_STEP):
        ge = jnp.sum(jnp.where(lane == e * MOE_EXPERTS_PER_STEP + k, gates, 0.0), axis=1, keepdims=True)
        a = _dot(hb, wg_ref[k])
        u = _dot(hb, wu_ref[k])
        hids.append((a * jax.nn.sigmoid(a) * u * ge).astype(BF))
    wd = wd_ref[...]
    acc_s[...] += _dot(jnp.concatenate(hids, axis=1), wd.reshape(wd.shape[0] * wd.shape[1], wd.shape[2]))

    @pl.when(e == pl.num_programs(1) - 1)
    def _():
        o_ref[...] = x_ref[...] + gt_ref[...] * acc_s[...]


def moe_layer(x2, g, sc, sh, gt, wr, br, wg, wu, wd, l, *, tm, mod_rows):
    N, D = x2.shape
    Hh = wg.shape[-1]
    row = pl.BlockSpec((tm, D), lambda i, e: (i, 0))
    mod = lambda a: pl.BlockSpec((None, a.shape[1], D), lambda i, e: ((i * tm) // mod_rows, 0, 0))
    full = lambda a: pl.BlockSpec(a.shape, lambda i, e: (0,) * a.ndim)
    return pl.pallas_call(
        _moe_body,
        grid=(N // tm, MOE_EXPERTS // MOE_EXPERTS_PER_STEP),
        in_specs=[row, full(g), mod(sc), mod(sh), mod(gt), full(wr), full(br),
                  pl.BlockSpec((None, MOE_EXPERTS_PER_STEP, D, Hh), lambda i, e: (l, e, 0, 0)),
                  pl.BlockSpec((None, MOE_EXPERTS_PER_STEP, D, Hh), lambda i, e: (l, e, 0, 0)),
                  pl.BlockSpec((None, MOE_EXPERTS_PER_STEP, Hh, D), lambda i, e: (l, e, 0, 0))],
        out_specs=row,
        out_shape=jax.ShapeDtypeStruct((N, D), F32),
        scratch_shapes=[pltpu.VMEM((tm, D), BF), pltpu.VMEM((tm, LANES), F32), pltpu.VMEM((tm, D), F32)],
        compiler_params=_cparams(("parallel", "arbitrary")),
        name="moe",
    )(x2, g, sc, sh, gt, wr, br, wg, wu, wd)


def _softplus(z):
    return jnp.maximum(z, 0.0) + jnp.log(1.0 + jnp.exp(-jnp.abs(z)))


def _pflash_body(*refs, mode, gb, hg, tq, tk, nk, dv, window):
    it = iter(refs)
    q_ref, k_ref, v_ref = next(it), next(it), next(it)
    cq_ref = ck_ref = sel_ref = e_ref = u_ref = None
    if mode == "fox":
        cq_ref, ck_ref = next(it), next(it)
    if mode == "sel":
        sel_ref, e_ref = next(it), next(it)
    if mode == "sb":
        u_ref = next(it)
    o_ref, m_s, acc_s = next(it), next(it), next(it)
    i, j = pl.program_id(2), pl.program_id(3)
    R = hg * tq
    last = (i * tq + tq - 1) // tk

    @pl.when(j == 0)
    def _():
        m_s[...] = jnp.full(m_s.shape, 0.0 if mode == "sb" else -jnp.inf, F32)
        acc_s[...] = jnp.zeros_like(acc_s)

    if mode == "win":
        kt = last - (nk - 1) + j
        active = kt >= 0
        edge = active
    elif mode == "sb":
        kt = last - j
        active = kt >= 0
        edge = j == 0
    else:
        kt = j
        active = j <= last
        edge = j == last

    def tile(gi, positional):
        q = q_ref[gi].reshape(R, q_ref.shape[-1])
        s3 = _dot_t(q, k_ref[gi]).reshape(hg, tq, tk)
        mask = None
        if positional:
            qpos = i * tq + lax.broadcasted_iota(jnp.int32, (tq, tk), 0)
            kpos = kt * tk + lax.broadcasted_iota(jnp.int32, (tq, tk), 1)
            if mode == "sb":
                mask = kpos < qpos
            elif mode == "win":
                d = qpos - kpos
                mask = (d >= 0) & (d < window)
            else:
                mask = kpos <= qpos
        if mode == "sel":
            picked = _dot(sel_ref[gi], e_ref[...]) > 0.5
            mask = picked if mask is None else mask & picked
        if mode == "fox":
            s3 = s3 - ck_ref[gi]
        chunk = lambda a: [a[:, c * LANES:(c + 1) * LANES] for c in range(tk // LANES)]
        if mode == "sb":
            lk3 = -_softplus(s3)
            if mask is not None:
                lk3 = jnp.where(mask[None], lk3, 0.0)
            lk = lk3.reshape(R, tk)
            hi = lk.astype(BF)
            lo = (lk - hi.astype(F32)).astype(BF)
            u = u_ref[...]
            sufx = _dot(hi, u) + _dot(lo, u)
            carry = m_s[gi]
            e = jnp.concatenate([a + b + carry for a, b in zip(chunk((s3 + lk3).reshape(R, tk)), chunk(sufx))], axis=1)
            w3 = jnp.exp(e.reshape(hg, tq, tk))
            if mask is not None:
                w3 = jnp.where(mask[None], w3, 0.0)
            acc_s[gi] += _dot(w3.reshape(R, tk).astype(BF), v_ref[gi])
            m_s[gi] = carry + sufx[:, tk:]
        else:
            if mask is not None:
                s3 = jnp.where(mask[None], s3, -jnp.inf)
            chunks = chunk(s3.reshape(R, tk))
            if mode == "fox":
                cq = cq_ref[gi].reshape(R, LANES)
                chunks = [a + cq for a in chunks]
            m_prev = m_s[gi]
            m_new = jnp.maximum(m_prev, jnp.max(functools.reduce(jnp.maximum, chunks), axis=1, keepdims=True))
            m_use = jnp.where(m_new > -jnp.inf, m_new, 0.0)
            p = jnp.concatenate([jnp.exp(a - m_use) for a in chunks], axis=1).astype(BF)
            acc_s[gi] = jnp.exp(m_prev - m_use) * acc_s[gi] + _dot(p, v_ref[gi])
            m_s[gi] = m_new

    @pl.when(active & edge)
    def _():
        for gi in range(gb):
            tile(gi, True)

    if mode != "win":
        @pl.when(active & jnp.logical_not(edge))
        def _():
            for gi in range(gb):
                tile(gi, False)

    @pl.when(j == nk - 1)
    def _():
        acc = acc_s[...]
        out = acc[:, :, :dv]
        if mode != "sb":
            l = acc[:, :, dv:dv + 1]
            out = out / jnp.where(l > 0.0, l, 1.0)
        o_ref[...] = out.reshape(o_ref.shape).astype(o_ref.dtype)


def _ones_column(v):
    pad = LANES - v.shape[-1] - 1
    return jnp.concatenate([v, jnp.ones(v.shape[:-1] + (1,), v.dtype), jnp.zeros(v.shape[:-1] + (pad,), v.dtype)], axis=-1)


def pflash(q, k, v, *, mode, tq, tk, gb=2, extras=(), window=NSA_WINDOW):
    B, G, hg, T, dk = q.shape
    dv = v.shape[-1]
    tq, tk = _pick_tile(T, tq), _pick_tile(T, tk)
    assert tk % LANES == 0 and tk % tq == 0
    gb = _pick_tile(G, gb)
    nq = T // tq
    last = lambda i: (i * tq + tq - 1) // tk
    if mode == "win":
        nk = max(last(i) - max(i * tq - window + 1, 0) // tk + 1 for i in range(nq))
        kidx = lambda i, j: jnp.maximum(last(i) - (nk - 1) + j, 0)
    elif mode == "sb":
        nk = T // tk
        kidx = lambda i, j: jnp.maximum(last(i) - j, 0)
    else:
        nk = T // tk
        kidx = lambda i, j: jnp.minimum(j, last(i))
    in_specs = [pl.BlockSpec((None, gb, hg, tq, dk), lambda b, g, i, j: (b, g, 0, i, 0)),
                pl.BlockSpec((None, gb, tk, dk), lambda b, g, i, j: (b, g, kidx(i, j), 0)),
                pl.BlockSpec((None, gb, tk, LANES), lambda b, g, i, j: (b, g, kidx(i, j), 0))]
    if mode == "fox":
        in_specs += [pl.BlockSpec((None, gb, hg, tq, LANES), lambda b, g, i, j: (b, g, 0, i, 0)),
                     pl.BlockSpec((None, gb, hg, 1, tk), lambda b, g, i, j: (b, g, 0, 0, kidx(i, j)))]
    if mode == "sel":
        nbp = extras[0].shape[-1]
        in_specs += [pl.BlockSpec((None, gb, tq, nbp), lambda b, g, i, j: (b, g, i, 0)),
                     pl.BlockSpec((nbp, tk), lambda b, g, i, j: (0, kidx(i, j)))]
    if mode == "sb":
        extras = (jnp.asarray(np.concatenate([np.tril(np.ones((tk, tk), np.float32), -1), np.ones((tk, LANES), np.float32)],
                                             axis=1), dtype=BF),)
        in_specs += [pl.BlockSpec((tk, tk + LANES), lambda b, g, i, j: (0, 0))]
    R = hg * tq
    return pl.pallas_call(
        functools.partial(_pflash_body, mode=mode, gb=gb, hg=hg, tq=tq, tk=tk, nk=nk, dv=dv, window=window),
        grid=(B, G // gb, nq, nk),
        in_specs=in_specs,
        out_specs=pl.BlockSpec((None, gb, hg, tq, dv), lambda b, g, i, j: (b, g, 0, i, 0)),
        out_shape=jax.ShapeDtypeStruct((B, G, hg, T, dv), BF),
        scratch_shapes=[pltpu.VMEM((gb, R, LANES), F32), pltpu.VMEM((gb, R, LANES), F32)],
        compiler_params=_cparams(("parallel", "parallel", "parallel", "arbitrary")),
        name="pflash_" + mode,
    )(q, k, _ones_column(v), *extras)


def _cmpsel_body(q_ref, kc_ref, vc_ref, oc_ref, sel_ref, *, hg, tq, nbc, nbs, n_sel, pos0):
    for bi in range(q_ref.shape[0]):
        for g in range(q_ref.shape[1]):
            _cmpsel_chain(q_ref.at[bi, g], kc_ref.at[bi, g], vc_ref.at[bi, g], oc_ref.at[bi, g], sel_ref.at[bi, g],
                          hg=hg, tq=tq, nbc=nbc, nbs=nbs, n_sel=n_sel, pos0=pos0)


def _cmpsel_chain(q_ref, kc_ref, vc_ref, oc_ref, sel_ref, *, hg, tq, nbc, nbs, n_sel, pos0):
    qi = pl.program_id(1)
    R = hg * tq
    nbp = kc_ref.shape[0]
    s = _dot_t(q_ref[...], kc_ref[...], precision=HI)
    n_r = lax.broadcasted_iota(jnp.int32, (R, nbp), 1)
    r_r = lax.broadcasted_iota(jnp.int32, (R, nbp), 0)
    qpos_r = pos0 + qi * tq + (r_r % tq)
    cmask = (n_r * NSA_BLOCK + (NSA_BLOCK - 1) <= qpos_r) & (n_r < nbc)
    s = jnp.where(cmask, s, -jnp.inf)
    m = jnp.max(s, axis=1, keepdims=True)
    m = jnp.where(m > -jnp.inf, m, 0.0)
    ex = jnp.exp(s - m)
    den = jnp.sum(ex, axis=1, keepdims=True)
    p = ex / jnp.where(den > 0.0, den, 1.0)
    oc_ref[...] = _dot(p.astype(BF), vc_ref[...].astype(BF)).astype(oc_ref.dtype)
    tt = lax.broadcasted_iota(jnp.int32, (tq, R), 0)
    rr = lax.broadcasted_iota(jnp.int32, (tq, R), 1)
    fold = (rr % tq == tt).astype(BF)
    a, b, c = _split3(p)
    imp = _dot(fold, a) + _dot(fold, b) + _dot(fold, c)
    n = lax.broadcasted_iota(jnp.int32, (tq, nbp), 1)
    qpos = pos0 + qi * tq + lax.broadcasted_iota(jnp.int32, (tq, nbp), 0)
    cur = qpos // NSA_BLOCK
    forced = (n == 0) | (n == cur) | (n == cur - 1)
    score = jnp.where(forced, float(hg + 1), jnp.where(n <= cur, imp, -1.0))
    score = jnp.where(n < nbs, score, -2.0)
    nf = n.astype(F32)
    sel = jnp.zeros((tq, nbp), F32)
    for _ in range(n_sel):
        mx = jnp.max(score, axis=1, keepdims=True)
        idx = jnp.min(jnp.where(score == mx, nf, 1e9), axis=1, keepdims=True)
        hit = nf == idx
        sel = jnp.where(hit & (mx >= 0.0), 1.0, sel)
        score = jnp.where(hit, -3.0, score)
    sel_ref[...] = sel.astype(sel_ref.dtype)


def nsa_cmp_sel(qf, kc, vc, *, tq, nbc, nbs, pos0, bb):
    B, G, nqt, R, d = qf.shape
    hg = R // tq
    nbp = kc.shape[2]
    n_sel = min(NSA_TOPN, nbs)
    bb = _pick_tile(B, bb)
    return pl.pallas_call(
        functools.partial(_cmpsel_body, hg=hg, tq=tq, nbc=nbc, nbs=nbs, n_sel=n_sel, pos0=pos0),
        grid=(B // bb, nqt),
        in_specs=[pl.BlockSpec((bb, G, None, R, d), lambda b, i: (b, 0, i, 0, 0)),
                  pl.BlockSpec((bb, G, nbp, d), lambda b, i: (b, 0, 0, 0)),
                  pl.BlockSpec((bb, G, nbp, d), lambda b, i: (b, 0, 0, 0))],
        out_specs=[pl.BlockSpec((bb, G, None, R, d), lambda b, i: (b, 0, i, 0, 0)),
                   pl.BlockSpec((bb, G, None, tq, nbp), lambda b, i: (b, 0, i, 0, 0))],
        out_shape=[jax.ShapeDtypeStruct((B, G, nqt, R, d), BF), jax.ShapeDtypeStruct((B, G, nqt, tq, nbp), BF)],
        compiler_params=_cparams(("parallel", "parallel")),
        name="nsa_cmp_sel",
    )(qf, kc, vc)


PAGE = 128
DEC_PAGES_PER_STEP = 8
DEC_SEQS_PER_STEP = 4
NEW_PAD = PAGE


def _dec_softmax_step(s, vs, m_s, l_s, acc_s):
    m_prev = m_s[:, :1]
    m_new = jnp.maximum(m_prev, jnp.max(s, axis=1, keepdims=True))
    m_use = jnp.where(m_new > -jnp.inf, m_new, 0.0)
    p = jnp.exp(s - m_use)
    alpha = jnp.exp(m_prev - m_use)
    l_s[...] = jnp.broadcast_to(alpha * l_s[:, :1] + jnp.sum(p, axis=1, keepdims=True), l_s.shape)
    w = vs[0].shape[1]
    pv = _dot_t(p[:, :w].astype(BF), vs[0])
    for n in range(1, len(vs)):
        pv = pv + _dot_t(p[:, n * w:(n + 1) * w].astype(BF), vs[n])
    acc_s[...] = alpha * acc_s[...] + pv
    m_s[...] = jnp.broadcast_to(m_new, m_s.shape)


def _dec_body(pt_ref, *refs, mode, G, bb, nc, n_new, rows_per_t, t_period, wlen, window):
    del pt_ref
    it = iter(refs)
    q_ref = next(it)
    cq_ref = ck_ref = cknew_ref = selx_ref = e2_ref = selnew_ref = u_ref = None
    if mode == "fox":
        cq_ref, ck_ref, cknew_ref = next(it), next(it), next(it)
    if mode == "sel":
        selx_ref, e2_ref, selnew_ref = next(it), next(it), next(it)
    if mode == "sb":
        u_ref = next(it)
    pages = [[next(it) for _ in range(G)] for _ in range(bb)]
    new_ref = next(it)
    o_ref, m_s, l_s, acc_s = next(it), next(it), next(it), next(it)
    c = pl.program_id(1)
    R = q_ref.shape[1]

    def row_t(shape):
        r = lax.broadcasted_iota(jnp.int32, shape, 0)
        return (r % t_period) // rows_per_t

    def split_kv(x):
        if mode == "mla":
            kb = x.astype(BF)
            return kb, kb[:256]
        return x[:128].astype(BF), x[128:].astype(BF)

    def new_rows(bi):
        q = q_ref[bi]
        kb, vb = split_kv(new_ref[bi])
        s = _dot(q, kb)
        t = row_t(s.shape)
        key = lax.broadcasted_iota(jnp.int32, s.shape, 1)
        if mode == "sb":
            mask = (key < t) & (key < n_new)
            lk = jnp.where(mask, -_softplus(s), 0.0)
            hi = lk.astype(BF)
            lo = (lk - hi.astype(F32)).astype(BF)
            suf = _dot(hi, u_ref[...]) + _dot(lo, u_ref[...])
            w = jnp.where(mask, jnp.exp(s + lk + suf), 0.0)
            acc_s[bi] += _dot_t(w.astype(BF), vb)
            m_s[bi] = jnp.broadcast_to(jnp.sum(lk, axis=1, keepdims=True), (R, LANES))
        else:
            mask = (key <= t) & (key < n_new)
            if mode == "sel":
                mask = mask & (selnew_ref[bi] > 0.5)
            if mode == "fox":
                s = s + cq_ref[bi][:, :1] - cknew_ref[bi]
            _dec_softmax_step(jnp.where(mask, s, -jnp.inf), [vb], m_s.at[bi], l_s.at[bi], acc_s.at[bi])

    def past_pages(bi):
        q = q_ref[bi]
        kvs = [split_kv(p[...]) for p in pages[bi]]
        zs = [_dot(q, kb) for kb, _ in kvs]
        if mode == "sb":
            u = u_ref[...]
            carry = m_s[bi][:, :1]
            pv = None
            for z, (_, vb) in zip(zs, kvs):
                lk = -_softplus(z)
                hi = lk.astype(BF)
                lo = (lk - hi.astype(F32)).astype(BF)
                suf = _dot(hi, u) + _dot(lo, u)
                w = jnp.exp(z + lk + suf + carry)
                d = _dot_t(w.astype(BF), vb)
                pv = d if pv is None else pv + d
                carry = carry + jnp.sum(lk, axis=1, keepdims=True)
            acc_s[bi] += pv
            m_s[bi] = jnp.broadcast_to(carry, (R, LANES))
        else:
            s = jnp.concatenate(zs, axis=1) if G > 1 else zs[0]
            if mode == "fox":
                s = s + cq_ref[bi][:, :1] - ck_ref[bi]
            if mode == "sel":
                s = jnp.where(_dot(selx_ref[bi], e2_ref[...]) > 0.5, s, -jnp.inf)
            if mode == "win":
                widx = c * (G * PAGE) + lax.broadcasted_iota(jnp.int32, s.shape, 1)
                s = jnp.where(widx > row_t(s.shape) + (wlen - window), s, -jnp.inf)
            _dec_softmax_step(s, [vb for _, vb in kvs], m_s.at[bi], l_s.at[bi], acc_s.at[bi])

    @pl.when(c == 0)
    def _():
        m_s[...] = jnp.full(m_s.shape, 0.0 if mode == "sb" else -jnp.inf, F32)
        l_s[...] = jnp.zeros_like(l_s)
        acc_s[...] = jnp.zeros_like(acc_s)
        for bi in range(bb):
            new_rows(bi)

    for bi in range(bb):
        past_pages(bi)

    @pl.when(c == nc - 1)
    def _():
        if mode == "sb":
            o_ref[...] = acc_s[...]
        else:
            l = l_s[...][:, :, :1]
            o_ref[...] = acc_s[...] / jnp.where(l > 0.0, l, 1.0)


def dec_attn(pt, q, cache, new, *, mode, extras=(), n_new, rows_per_t, t_period, window=NSA_WINDOW):
    B, n_pages = pt.shape
    _, R, Dq = q.shape
    W = cache.shape[1]
    G = _pick_tile(n_pages, DEC_PAGES_PER_STEP)
    bb = _pick_tile(B, DEC_SEQS_PER_STEP)
    nc = n_pages // G
    dv = 256 if mode == "mla" else 128
    if mode == "sb":
        page_idx = lambda b, c, pt, bi, p: (pt[b * bb + bi, n_pages - 1 - (c * G + p)], 0, 0)
    elif mode == "win":
        page_idx = lambda b, c, pt, bi, p: (b * bb + bi, 0, c * G + p)
    else:
        page_idx = lambda b, c, pt, bi, p: (pt[b * bb + bi, c * G + p], 0, 0)
    per_b = lambda shape: pl.BlockSpec((bb,) + shape, lambda b, c, pt: (b,) + (0,) * len(shape))
    in_specs = [per_b((R, Dq))]
    if mode == "fox":
        in_specs += [per_b((R, LANES)), pl.BlockSpec((bb, R, G * PAGE), lambda b, c, pt: (b, 0, c)), per_b((R, NEW_PAD))]
    if mode == "sel":
        in_specs += [pl.BlockSpec((bb, None, R, 2 * G), lambda b, c, pt: (b, c, 0, 0)),
                     pl.BlockSpec((2 * G, G * PAGE), lambda b, c, pt: (0, 0)), per_b((R, NEW_PAD))]
    if mode == "sb":
        in_specs += [pl.BlockSpec((PAGE, PAGE), lambda b, c, pt: (0, 0))]
    in_specs += [pl.BlockSpec((None, W, PAGE), functools.partial(page_idx, bi=bi, p=p)) for bi in range(bb) for p in range(G)]
    in_specs += [per_b((W, NEW_PAD))]
    return pl.pallas_call(
        functools.partial(_dec_body, mode=mode, G=G, bb=bb, nc=nc, n_new=n_new, rows_per_t=rows_per_t, t_period=t_period,
                          wlen=n_pages * PAGE, window=window),
        grid_spec=pltpu.PrefetchScalarGridSpec(
            num_scalar_prefetch=1,
            grid=(B // bb, nc),
            in_specs=in_specs,
            out_specs=per_b((R, dv)),
            scratch_shapes=[pltpu.VMEM((bb, R, LANES), F32), pltpu.VMEM((bb, R, LANES), F32), pltpu.VMEM((bb, R, dv), F32)],
        ),
        out_shape=jax.ShapeDtypeStruct((B, R, dv), F32),
        compiler_params=_cparams(("parallel", "arbitrary")),
        name="dec_" + mode,
    )(pt, q, *extras, *([cache] * (bb * G)), new)


def _page_rows_body(pt_ref, *refs, G):
    del pt_ref
    o_ref = refs[G]
    o_ref[...] = jnp.concatenate([r[...] for r in refs[:G]], axis=0)


def gather_page_rows(pt, cache_rows):
    B, n_pages = pt.shape
    _, H, W = cache_rows.shape
    G = _pick_tile(n_pages, 16)
    return pl.pallas_call(
        functools.partial(_page_rows_body, G=G),
        grid_spec=pltpu.PrefetchScalarGridSpec(
            num_scalar_prefetch=1,
            grid=(B, n_pages // G),
            in_specs=[pl.BlockSpec((None, H, W), lambda b, c, pt, p=p: (pt[b, c * G + p], 0, 0)) for p in range(G)],
            out_specs=pl.BlockSpec((None, G * H, W), lambda b, c, pt: (b, c, 0)),
        ),
        out_shape=jax.ShapeDtypeStruct((B, n_pages * H, W), cache_rows.dtype),
        compiler_params=_cparams(("parallel", "parallel")),
        name="gather_page_rows",
    )(pt, *([cache_rows] * G))


def _cmp_pages_body(pt_ref, *refs, G, nc, n_pages):
    del pt_ref
    pages = refs[:G]
    w_ref, b_ref, o_ref, x_s = refs[G:]
    c = pl.program_id(1)
    F = pages[0].shape[0]
    for p in range(G):
        x_s[pl.ds(pl.multiple_of((c * G + p) * F, F), F), :] = pages[p][...]

    @pl.when(c == nc - 1)
    def _():
        for kv in range(2):
            for g in range(NSA_GROUPS):
                f0 = (kv * NSA_GROUPS + g) * HEAD_DIM
                x = jnp.concatenate([x_s[pl.ds(f0 + d, n_pages, stride=F), :] for d in range(HEAD_DIM)], axis=1)
                col = (kv * NSA_GROUPS + g) * LANES
                o_ref[:, col:col + LANES] = _dot(x.astype(BF), w_ref[kv]) + b_ref[:, col:col + LANES]


def compress_pages(pt, cache_t, w2, bias):
    B, n_pages = pt.shape
    F = cache_t.shape[1]
    G = _pick_tile(n_pages, 8)
    nc = n_pages // G
    return pl.pallas_call(
        functools.partial(_cmp_pages_body, G=G, nc=nc, n_pages=n_pages),
        grid_spec=pltpu.PrefetchScalarGridSpec(
            num_scalar_prefetch=1,
            grid=(B, nc),
            in_specs=[pl.BlockSpec((None, F, PAGE), lambda b, c, pt, p=p: (pt[b, c * G + p], 0, 0)) for p in range(G)]
            + [pl.BlockSpec(w2.shape, lambda b, c, pt: (0, 0, 0)), pl.BlockSpec(bias.shape, lambda b, c, pt: (0, 0))],
            out_specs=pl.BlockSpec((None, n_pages, 4 * LANES), lambda b, c, pt: (b, 0, 0)),
            scratch_shapes=[pltpu.VMEM((n_pages * F, PAGE), F32)],
        ),
        out_shape=jax.ShapeDtypeStruct((B, n_pages, 4 * LANES), F32),
        compiler_params=_cparams(("parallel", "arbitrary")),
        name="compress_pages",
    )(pt, *([cache_t] * G), w2, bias)


def _page_prefix_body(within_ref, tot_ref, l_ref, o_ref):
    a, b, c = _split3(tot_ref[...])
    l = l_ref[...]
    o_ref[...] = within_ref[...] + _dot(l, a) + _dot(l, b) + _dot(l, c)


def page_prefix(y, lstrict):
    B, npg, W2 = y.shape
    W = W2 // 2
    return pl.pallas_call(
        _page_prefix_body,
        grid=(B,),
        in_specs=[pl.BlockSpec((None, npg, W), lambda b: (b, 0, 0)), pl.BlockSpec((None, npg, W), lambda b: (b, 0, 1)),
                  pl.BlockSpec((npg, npg), lambda b: (0, 0))],
        out_specs=pl.BlockSpec((None, npg, W), lambda b: (b, 0, 0)),
        out_shape=jax.ShapeDtypeStruct((B, npg, W), F32),
        compiler_params=_cparams(("parallel",)),
        name="page_prefix",
    )(y, y, lstrict)


def _swap_halves(w, hd):
    K, N = w.shape
    return w.reshape(K, N // hd, 2, hd // 2)[:, :, ::-1, :].reshape(K, N)


def _pad_last(a, n):
    return jnp.pad(a, [(0, 0)] * (a.ndim - 1) + [(0, n - a.shape[-1])])


def _rope_tabs(pos, d):
    inv = ROPE_THETA ** (-jnp.arange(0, d, 2, dtype=F32) / d)
    ang = pos.astype(F32)[:, None] * inv[None, :]
    c, s = jnp.cos(ang), jnp.sin(ang)
    return jnp.concatenate([c, c], axis=1), jnp.concatenate([-s, s], axis=1)


def _prep_weights(W):
    D = W["ev_w_in"].shape[1]
    P = {}
    w = W["ev_w_in"][0]
    kr = w[:, 1280:1312]
    P["ev_wext"] = jnp.concatenate([w[:, :1280], _pad_last(kr, 128), _pad_last(_swap_halves(kr, 32), 128)], axis=1).astype(BF)
    wuq = W["ev_w_uq"][0]
    wn, wr = wuq[:, :, :MLA_NOPE].reshape(256, 512), wuq[:, :, MLA_NOPE:].reshape(256, 256)
    P["ev_wuq"] = jnp.concatenate([wn, wr, _swap_halves(wr, MLA_ROPE)], axis=1).astype(BF)
    wuk, wuv = W["ev_w_uk"][0], W["ev_w_uv"][0]
    eye8 = jnp.eye(MLA_HEADS, dtype=F32)
    P["ev_wuk"] = wuk.reshape(256, 512).astype(BF)
    P["ev_wuv"] = wuv.reshape(256, 512).astype(BF)
    P["ev_wabs"] = jnp.einsum("chn,hH->hnHc", wuk, eye8).reshape(512, 2048).astype(BF)
    P["ev_wuv_bd"] = jnp.einsum("chv,hH->hcHv", wuv, eye8).reshape(2048, 512).astype(BF)
    P["ev_wout"] = W["ev_w_out"][0].astype(BF)
    P["ev_gq"] = W["ev_q_norm"][0][None, :]
    P["ev_gkv"] = W["ev_kv_norm"][0][None, :]

    w = W["od_w_in"][0]
    seg = lambda a, b: w[:, a:b]
    nq, ck, cv, sk, sv, wk, wv = seg(0, 512), seg(512, 640), seg(640, 768), seg(768, 896), seg(896, 1024), seg(1024, 1152), seg(1152, 1280)
    gl, fq, fk, fv, fl = seg(1280, 1304), seg(1304, 1816), seg(1816, 1944), seg(1944, 2072), seg(2072, 2080)
    sw = lambda a: _swap_halves(a, HEAD_DIM)
    P["od_wext"] = jnp.concatenate([nq, sw(nq), ck, sw(ck), cv, sk, sw(sk), sv, wk, sw(wk), wv, _pad_last(gl, 128), fq, fk, fv,
                                    _pad_last(fl, 128)], axis=1).astype(BF)
    P["od_fb"] = _pad_last(W["od_fox_fb"][0][None, :], 128)
    P["od_wout"] = W["od_w_out"][0].astype(BF)
    eye2 = jnp.eye(2, dtype=F32)
    wbig = jnp.einsum("klde,kK,gG->lkgdKGe", W["od_cmp_w"][0], eye2, eye2).reshape(NSA_BLOCK * 256, 256)
    pe_flat = jnp.broadcast_to(W["od_cmp_pe"][0].transpose(1, 0, 2)[:, :, None, :], (NSA_BLOCK, 2, 2, HEAD_DIM)).reshape(1, -1)
    P["cmp_w"] = wbig.astype(BF)
    P["cmp_b"] = linear(jnp.pad(pe_flat, ((0, 7), (0, 0))), wbig, mode="hi", tn=128, name="cmp_bias")[:1]
    P["cmp_w2"] = jnp.einsum("klde,hH->kdhlHe", W["od_cmp_w"][0], eye2).reshape(2, HEAD_DIM * PAGE, 2 * HEAD_DIM).astype(BF)
    P["cmp_b2"] = jnp.broadcast_to(P["cmp_b"].reshape(2, 2, 1, HEAD_DIM), (2, 2, 2, HEAD_DIM)).reshape(1, 8 * HEAD_DIM)
    r = np.arange(128)
    cidx = np.arange(1536)
    P["gate_expand"] = jnp.asarray(((r[:, None] // 3 == (cidx[None, :] % 512) // 64) & (r[:, None] % 3 == cidx[None, :] // 512)
                                    & (r[:, None] < 24)).astype(np.float32), dtype=BF)
    i = np.arange(PAGE)
    P["scan_w"] = jnp.asarray(np.concatenate([i[:, None] <= i[None, :], np.ones((PAGE, PAGE), bool)], axis=1).astype(np.float32),
                              dtype=BF)
    P["moe_wg"] = W["moe_w_gate"].astype(BF)
    P["moe_wu"] = W["moe_w_up"].astype(BF)
    P["moe_wd"] = W["moe_w_down"].astype(BF)
    return P


def _mods(c, W, l, T, per_token):
    B, D = c.shape
    m = linear(c, W["ada_w"], W["ada_b"].reshape(W["ada_b"].shape[0], 1, -1), widx=l, act="silu", name="ada_mod").reshape(B, 6, D)
    if per_token:
        return [jnp.repeat(m[:, j], T, axis=0)[None] for j in range(6)]
    return [m[:, j][:, None, :] for j in range(6)]


def _heads_major(a, B, T, G, hg):
    return a.reshape(B, T, G, hg, HEAD_DIM).transpose(0, 2, 3, 1, 4)


def _kv_major(a, B, T, G):
    return a.reshape(B, T, G, HEAD_DIM).transpose(0, 2, 1, 3)


def _from_heads_major(o, N):
    B, G, hg, T, d = o.shape
    return o.transpose(0, 3, 1, 2, 4).reshape(N, G * hg * d)


def _block_diag_q(q, B, T):
    q5 = q.reshape(B, T, 2, 4, HEAD_DIM).transpose(0, 2, 1, 3, 4).reshape(B, 2, T * 4, 1, HEAD_DIM)
    eye = jnp.eye(2, dtype=q.dtype)[None, :, None, :, None]
    return (q5 * eye).reshape(B, 2 * T * 4, 2 * HEAD_DIM)


def _pick_diag(o, B, T):
    o6 = o.reshape(B, 2, T, 4, 2, HEAD_DIM)
    return jnp.stack([o6[:, 0, :, :, 0], o6[:, 1, :, :, 1]], axis=2).reshape(B * T, 512)


def _pad_new(rows, B, T):
    return jnp.pad(rows.reshape(B, T, -1), ((0, 0), (0, NEW_PAD - T), (0, 0))).transpose(0, 2, 1)


def _token_minor(cache):
    n, t = cache.shape[:2]
    return jnp.moveaxis(cache, 1, -1).reshape(n, -1, t)


def _head_rows(T):
    kv, t, h = np.meshgrid(np.arange(2), np.arange(T), np.arange(4), indexing="ij")
    return (kv * 4 + h).reshape(-1), t.reshape(-1)


def _scan_logf(x_rows, P):
    B, rows, _ = x_rows.shape
    npg = rows // 8
    y = linear(x_rows.reshape(B * rows, PAGE), P["scan_w"], mode="split3", tn=2 * PAGE, name="scan_within")
    r = np.arange(rows)
    earlier = jnp.asarray(((r[:, None] // 8 > r[None, :] // 8) & (r[:, None] % 8 == r[None, :] % 8)).astype(np.float32), dtype=BF)
    cs = page_prefix(y.reshape(B, rows, 2 * PAGE), earlier)
    return cs.reshape(B, npg, 8, PAGE).transpose(0, 2, 1, 3).reshape(B, 8, npg * PAGE)


def _trunk(x, c, W, P, past):
    B, T, D = x.shape
    N = B * T
    sample = past is not None
    x2 = x.reshape(N, D)
    tm = N if sample else _pick_tile(T, 256)
    mod_rows = N if sample else T
    kw = dict(tm=tm, mod_rows=mod_rows)
    kw_moe = dict(tm=N if sample else _pick_tile(T, MOE_TOKEN_TILE), mod_rows=mod_rows)
    pos0 = past["len"] if sample else 0
    pos = pos0 + jnp.arange(T, dtype=jnp.int32)
    rep = (lambda a: jnp.tile(a, (B, 1))) if sample else (lambda a: a)
    tab_rows = N if sample else T
    out = {}

    sh_a, sc_a, gt_a, sh_m, sc_m, gt_m = _mods(c, W, 0, T, sample)
    c32, s32 = _rope_tabs(pos, MLA_ROPE)
    tabs = [rep(_pad_last(c32, 128)), rep(_pad_last(s32, 128)), rep(jnp.tile(c32, (1, 8))), rep(jnp.tile(s32, (1, 8)))]
    g0 = W["norm_mix"][0][None, :]
    if not sample:
        sbq, sbrows, mla, qn, qr, kn, vv = even_front(x2, g0, sc_a, sh_a, P["ev_wext"], tabs, P["ev_gq"], P["ev_gkv"], P["ev_wuq"],
                                                       P["ev_wuk"], P["ev_wuv"], absorbed=False, tab_rows=tab_rows, **kw)
        sbb = sbrows.astype(BF)
        o_sb = pflash(_heads_major(sbq, B, T, 2, 4), _kv_major(sbb[:, :128], B, T, 2), _kv_major(sbb[:, 128:], B, T, 2),
                      mode="sb", **ATT_TILES)
        krb = jnp.broadcast_to(mla[:, 256:].astype(BF).reshape(B, T, 1, MLA_ROPE), (B, T, MLA_HEADS, MLA_ROPE))
        qm = jnp.concatenate([qn.reshape(B, T, 8, 64), qr.reshape(B, T, 8, 32)], axis=-1).transpose(0, 2, 1, 3)[:, :, None]
        km = jnp.concatenate([kn.reshape(B, T, 8, 64), krb], axis=-1).transpose(0, 2, 1, 3)
        vm = vv.reshape(B, T, 8, 64).transpose(0, 2, 1, 3)
        o_mla = pflash(qm, km, vm, mode="causal", **MLA_TILES)
        mix = jnp.concatenate([_from_heads_major(o_sb, N), _from_heads_major(o_mla, N)], axis=1)
    else:
        pt = past["pt"]
        sbq, sbrows, mla, qlat, qr = even_front(x2, g0, sc_a, sh_a, P["ev_wext"], tabs, P["ev_gq"], P["ev_gkv"], P["ev_wuq"],
                                                P["ev_wabs"], None, absorbed=True, tab_rows=tab_rows, **kw)
        u = jnp.asarray(np.tril(np.ones((PAGE, PAGE), np.float32), -1), dtype=BF)
        n_pool = past["sb"].shape[0]
        o = dec_attn(pt, _block_diag_q(sbq, B, T), _token_minor(past["sb"]), _pad_new(sbrows, B, T), mode="sb",
                     extras=(u,), n_new=T, rows_per_t=4, t_period=4 * T)
        o_sb = _pick_diag(o, B, T).astype(BF)
        qcat = jnp.concatenate([qlat.reshape(B, T, 8, 256), qr.reshape(B, T, 8, 32)], axis=-1).reshape(B, T * 8, 288)
        o_lat = dec_attn(pt, qcat, _token_minor(past["mla"]), _pad_new(mla, B, T), mode="mla", n_new=T,
                         rows_per_t=8, t_period=8 * T)
        o_mla = linear(o_lat.reshape(N, 2048), P["ev_wuv_bd"], out_dtype=BF, name="mla_v_up")
        mix = jnp.concatenate([o_sb, o_mla], axis=1)
    out["sb"] = sbrows.reshape(1, B, T, 2, SB_KV_HEADS, HEAD_DIM)
    out["mla"] = mla.reshape(1, B, T, 288)
    x2 = out_proj_res(mix, P["ev_wout"], x2, gt_a, **kw)
    x2 = moe_layer(x2, W["norm_ffn"][0][None, :], sc_m, sh_m, gt_m, P["wr"][0], P["br"][0],
                   P["moe_wg"], P["moe_wu"], P["moe_wd"], 0, **kw_moe)

    sh_a, sc_a, gt_a, sh_m, sc_m, gt_m = _mods(c, W, 1, T, sample)
    c64, s64 = _rope_tabs(pos, HEAD_DIM)
    tabs = [rep(jnp.tile(c64, (1, 8))), rep(jnp.tile(s64, (1, 8))), rep(jnp.tile(c64, (1, 2))), rep(jnp.tile(s64, (1, 2)))]
    nq, nqf, cmp_rows, slc_rows, win_rows, gates, fq, fox_rows, logf = odd_front(
        x2, W["norm_mix"][1][None, :], sc_a, sh_a, P["od_wext"], tabs, P["od_fb"], tab_rows=tab_rows, **kw)
    logf8 = logf[:, :8]
    G, hg = NSA_GROUPS, NSA_HEADS // NSA_GROUPS
    if not sample:
        nb = T // NSA_BLOCK
        nbp = _round_up(nb, LANES)
        kcv = linear(cmp_rows.astype(BF).reshape(B * nb, NSA_BLOCK * 256), P["cmp_w"], P["cmp_b"], tm=128, tn=256, name="compress")
        kcv = kcv.reshape(B, nb, 2, G, HEAD_DIM).transpose(2, 0, 3, 1, 4)
        kcv = jnp.pad(kcv, ((0, 0), (0, 0), (0, 0), (0, nbp - nb), (0, 0)))
        tq = _pick_tile(T, 128)
        nqt = T // tq
        qf = nqf.reshape(B, nqt, tq, G, hg, HEAD_DIM).transpose(0, 3, 1, 4, 2, 5).reshape(B, G, nqt, hg * tq, HEAD_DIM)
        oc, sel = nsa_cmp_sel(qf, kcv[0], kcv[1], tq=tq, nbc=nb, nbs=nb, pos0=0, bb=2)
        oc = oc.reshape(B, G, nqt, hg, tq, HEAD_DIM).transpose(0, 2, 4, 1, 3, 5).reshape(N, 512)
        sel = sel.reshape(B, G, T, nbp)
        expand = jnp.asarray((np.arange(nbp)[:, None] == np.arange(T)[None, :] // NSA_BLOCK).astype(np.float32), dtype=BF)
        q5 = _heads_major(nq, B, T, G, hg)
        slb, wnb, fxb = slc_rows.astype(BF), win_rows.astype(BF), fox_rows.astype(BF)
        osel = pflash(q5, _kv_major(slb[:, :128], B, T, G), _kv_major(slb[:, 128:], B, T, G), mode="sel", extras=(sel, expand), **ATT_TILES)
        ow = pflash(q5, _kv_major(wnb[:, :128], B, T, G), _kv_major(wnb[:, 128:], B, T, G), mode="win", **ATT_TILES)
        cs = _scan_logf(logf8.reshape(B, T // PAGE, PAGE, 8).transpose(0, 1, 3, 2).reshape(B, T // PAGE * 8, PAGE), P)
        ch = cs.reshape(B, 2, 4, T)
        cq = jnp.broadcast_to(ch[..., None], (B, 2, 4, T, LANES))
        of = pflash(_heads_major(fq, B, T, 2, 4), _kv_major(fxb[:, :128], B, T, 2), _kv_major(fxb[:, 128:], B, T, 2), mode="fox",
                    extras=(cq, ch[:, :, :, None, :]), **ATT_TILES)
        osel, ow, of = _from_heads_major(osel, N), _from_heads_major(ow, N), _from_heads_major(of, N)
        wb = min(NSA_WINDOW, T)
        out["win"] = win_rows.reshape(1, B, T, 2, G, HEAD_DIM)[:, :, T - wb:]
    else:
        n_pages = pt.shape[1]
        L = n_pages * PAGE
        nbc = L // NSA_BLOCK
        nbs = nbc + 1
        nbp = _round_up(nbs, LANES)
        kcv = compress_pages(pt, _token_minor(past["cmp"]), P["cmp_w2"], P["cmp_b2"])
        kcv = kcv.reshape(B, n_pages, 2, G, 2, HEAD_DIM).transpose(2, 0, 3, 1, 4, 5).reshape(2, B, G, nbc, HEAD_DIM)
        kcv = jnp.pad(kcv, ((0, 0), (0, 0), (0, 0), (0, nbp - nbc), (0, 0)))
        qf = nqf.reshape(B, T, G, hg, HEAD_DIM).transpose(0, 2, 3, 1, 4).reshape(B, G, 1, hg * T, HEAD_DIM)
        oc, sel = nsa_cmp_sel(qf, kcv[0], kcv[1], tq=T, nbc=nbc, nbs=nbs, pos0=pos0, bb=4)
        oc = oc.reshape(B, G, hg, T, HEAD_DIM).transpose(0, 3, 1, 2, 4).reshape(N, 512)
        sel = sel.reshape(B, G, T, nbp)
        Gp = _pick_tile(n_pages, DEC_PAGES_PER_STEP)
        nc = n_pages // Gp
        selr = jnp.broadcast_to(sel[:, :, :, None, :], (B, G, T, hg, nbp)).reshape(B, G * T * hg, nbp)
        selx = selr[:, :, :2 * n_pages].reshape(B, G * T * hg, nc, 2 * Gp).transpose(0, 2, 1, 3)
        selnew = jnp.broadcast_to(selr[:, :, 2 * n_pages:2 * n_pages + 1].astype(F32), (B, G * T * hg, NEW_PAD))
        e2 = jnp.asarray((np.arange(2 * Gp)[:, None] == np.arange(Gp * PAGE)[None, :] // NSA_BLOCK).astype(np.float32), dtype=BF)
        rk = dict(n_new=T, rows_per_t=4, t_period=4 * T)
        qbd = _block_diag_q(nq, B, T)
        osel = dec_attn(pt, qbd, _token_minor(past["slc"]), _pad_new(slc_rows, B, T), mode="sel",
                        extras=(selx, e2, selnew), **rk)
        win_buf = past["win"]
        npw = win_buf.shape[1] // PAGE
        pt_win = jnp.zeros((B, npw), jnp.int32)
        ow = dec_attn(pt_win, qbd, _token_minor(win_buf), _pad_new(win_rows, B, T), mode="win", **rk)
        cpast = _scan_logf(gather_page_rows(pt, _token_minor(past["logf"])), P)
        lnew = logf8.reshape(B, T, 8)
        run = cpast[:, :, -1]
        cnew = []
        for tt in range(T):
            run = run + lnew[:, tt, :]
            cnew.append(run)
        cnew = jnp.stack(cnew, axis=1)
        hrow, trow = _head_rows(T)
        ck32 = cpast[:, hrow, :]
        cq32 = jnp.broadcast_to(cnew[:, trow, hrow][:, :, None], (B, len(hrow), LANES))
        cknew = _pad_last(cnew.transpose(0, 2, 1)[:, hrow, :], NEW_PAD)
        of = dec_attn(pt, _block_diag_q(fq, B, T), _token_minor(past["fox"]), _pad_new(fox_rows, B, T), mode="fox",
                      extras=(cq32, ck32, cknew), **rk)
        osel, ow, of = (_pick_diag(a, B, T).astype(BF) for a in (osel, ow, of))
        out["win"] = jnp.concatenate([win_buf, win_rows.reshape(B, T, 2, G, HEAD_DIM)], axis=1)[None, :, T:]
    mix = nsa_combine(oc, osel, ow, gates, P["gate_expand"], of, tm=tm)
    out["cmp"] = cmp_rows.reshape(1, B, T, 2, G, HEAD_DIM)
    out["slc"] = slc_rows.reshape(1, B, T, 2, G, HEAD_DIM)
    out["fox"] = fox_rows.reshape(1, B, T, 2, FOX_KV_HEADS, HEAD_DIM)
    out["logf"] = logf8.reshape(1, B, T, FOX_HEADS)
    x2 = out_proj_res(mix, P["od_wout"], x2, gt_a, **kw)
    x2 = moe_layer(x2, W["norm_ffn"][1][None, :], sc_m, sh_m, gt_m, P["wr"][1], P["br"][1],
                   P["moe_wg"], P["moe_wu"], P["moe_wd"], 1, **kw_moe)
    y = final_norm(x2, W["norm_final"][None, :], tm=tm).reshape(B, T, D)
    return y, out


def kernel(x_prompt, x_sample, c_prompt, c_sample, cache_sb_kv, cache_mla_latent, cache_nsa_cmp_kv, cache_nsa_slc_kv,
           cache_fox_kv, cache_fox_logf, state_nsa_win_kv, page_table, ada_w, ada_b, norm_mix, norm_ffn, norm_final,
           ev_w_in, ev_q_norm, ev_kv_norm, ev_w_uq, ev_w_uk, ev_w_uv, ev_w_out, od_w_in, od_cmp_pe, od_cmp_w, od_fox_fb,
           od_w_out, moe_w_grp, moe_b_grp, moe_w_rt, moe_b_rt, moe_w_gate, moe_w_up, moe_w_down):
    assert ada_w.shape[0] == 2, "two trunk layers (one even, one odd) are implemented"
    W = dict(ada_w=ada_w, ada_b=ada_b, norm_mix=norm_mix, norm_ffn=norm_ffn, norm_final=norm_final, ev_w_in=ev_w_in,
             ev_q_norm=ev_q_norm, ev_kv_norm=ev_kv_norm, ev_w_uq=ev_w_uq, ev_w_uk=ev_w_uk, ev_w_uv=ev_w_uv, ev_w_out=ev_w_out,
             od_w_in=od_w_in, od_cmp_pe=od_cmp_pe, od_cmp_w=od_cmp_w, od_fox_fb=od_fox_fb, od_w_out=od_w_out,
             moe_w_gate=moe_w_gate, moe_w_up=moe_w_up, moe_w_down=moe_w_down)
    P = _prep_weights(W)
    P["wr"] = _pad_last(jnp.concatenate([moe_w_rt, moe_w_grp], axis=-1), LANES)
    P["br"] = _pad_last(jnp.concatenate([moe_b_rt, moe_b_grp], axis=-1), LANES)[:, None, :]
    past = dict(pt=page_table, len=page_table.shape[1] * cache_sb_kv.shape[2], sb=cache_sb_kv[0], mla=cache_mla_latent[0],
                cmp=cache_nsa_cmp_kv[0], slc=cache_nsa_slc_kv[0], fox=cache_fox_kv[0], logf=cache_fox_logf[0],
                win=state_nsa_win_kv[0])
    y_p, sp = _trunk(x_prompt, c_prompt, W, P, None)
    y_s, ss = _trunk(x_sample, c_sample, W, P, past)
    names = ("sb", "mla", "cmp", "slc", "fox", "logf", "win")
    outs = [y_p, y_s]
    for n in names:
        outs += [sp[n], ss[n]]
    return tuple(outs)
```

```python
import functools
import math

import jax
import jax.numpy as jnp
import numpy as np
from jax import lax
from jax.experimental import pallas as pl
from jax.experimental.pallas import tpu as pltpu

F32 = jnp.float32
BF = jnp.bfloat16
HI = lax.Precision.HIGHEST

HEAD_DIM = 64
ROPE_THETA = 10000.0
RMS_EPS = 1e-6
SB_HEADS, SB_KV_HEADS = 8, 2
MLA_HEADS, MLA_NOPE, MLA_ROPE, MLA_V = 8, 64, 32, 64
NSA_HEADS, NSA_GROUPS, NSA_BLOCK, NSA_TOPN, NSA_WINDOW = 8, 2, 64, 16, 512
FOX_HEADS, FOX_KV_HEADS = 8, 2
MOE_GROUPS, MOE_PER_GROUP, MOE_TOPK = 4, 8, 2
MOE_EXPERTS = MOE_GROUPS * MOE_PER_GROUP
MLA_SCALE = 1.0 / math.sqrt(MLA_NOPE + MLA_ROPE)
QK_SCALE = 1.0 / math.sqrt(HEAD_DIM)
LANES = 128
VMEM_LIMIT_MB = 56
ATT_TILES = dict(tq=256, tk=256)
MLA_TILES = dict(tq=512, tk=512, gb=4)
MOE_TOKEN_TILE = 1024
MOE_EXPERTS_PER_STEP = 4


def _cparams(sem, vmem_mb=VMEM_LIMIT_MB):
    return pltpu.CompilerParams(dimension_semantics=sem, vmem_limit_bytes=vmem_mb * 2 ** 20)


def _round_up(a, m):
    return (a + m - 1) // m * m


def _pick_tile(n, pref):
    t = min(pref, n)
    while n % t:
        t //= 2
    return t


def _rms(x, g):
    return x * lax.rsqrt(jnp.mean(x * x, axis=-1, keepdims=True) + RMS_EPS) * g


def _dot(a, b):
    return jnp.dot(a, b, preferred_element_type=F32)


def _dot_t(a, b, precision=None):
    return lax.dot_general(a, b, (((1,), (1,)), ((), ())), preferred_element_type=F32, precision=precision)


def _split3(x):
    a = x.astype(BF)
    r = x - a.astype(F32)
    b = r.astype(BF)
    c = (r - b.astype(F32)).astype(BF)
    return a, b, c


def _linear_body(*refs, act, mode, has_bias):
    x_ref, w_ref = refs[0], refs[1]
    b_ref = refs[2] if has_bias else None
    o_ref = refs[-1]
    x = x_ref[...]
    if act == "silu":
        x = x.astype(F32)
        x = x * jax.nn.sigmoid(x)
    w = w_ref[...]
    if mode == "hi":
        y = jnp.dot(x.astype(F32), w.astype(F32), precision=HI, preferred_element_type=F32)
    elif mode == "split3":
        a, b, c = _split3(x.astype(F32))
        wb = w.astype(BF)
        y = _dot(a, wb) + _dot(b, wb) + _dot(c, wb)
    else:
        y = _dot(x.astype(BF), w.astype(BF))
    if has_bias:
        y = y + b_ref[...]
    o_ref[...] = y.astype(o_ref.dtype)


def linear(x, w, bias=None, *, widx=None, act=None, mode="bf16", out_dtype=F32, tm=256, tn=512, name="linear"):
    M, K = x.shape
    N = w.shape[-1]
    tm = _pick_tile(M, tm)
    tn = _pick_tile(N, tn)
    if w.ndim == 3:
        w_spec = pl.BlockSpec((None, K, tn), lambda i, j: (widx, 0, j))
    else:
        w_spec = pl.BlockSpec((K, tn), lambda i, j: (0, j))
    in_specs = [pl.BlockSpec((tm, K), lambda i, j: (i, 0)), w_spec]
    args = [x, w]
    if bias is not None:
        if bias.ndim == 3:
            in_specs.append(pl.BlockSpec((None, 1, tn), lambda i, j: (widx, 0, j)))
        else:
            in_specs.append(pl.BlockSpec((1, tn), lambda i, j: (0, j)))
        args.append(bias)
    return pl.pallas_call(
        functools.partial(_linear_body, act=act, mode=mode, has_bias=bias is not None),
        grid=(M // tm, N // tn),
        in_specs=in_specs,
        out_specs=pl.BlockSpec((tm, tn), lambda i, j: (i, j)),
        out_shape=jax.ShapeDtypeStruct((M, N), out_dtype),
        compiler_params=_cparams(("parallel", "parallel")),
        name=name,
    )(*args)


def _mod_spec(tm, D, mod_rows):
    return lambda R: pl.BlockSpec((None, R, D), lambda i: ((i * tm) // mod_rows, 0, 0))


def _even_front_body(x_ref, g_ref, sc_ref, sh_ref, w_ref, c32_ref, s32_ref, c256_ref, s256_ref, gq_ref, gkv_ref,
                     wuq_ref, wa_ref, wb_ref, sbq_ref, sbrows_ref, mla_ref, o1_ref, o2_ref, o3_ref, o4_ref, o5_ref, *, absorbed):
    hn = _rms(x_ref[...], g_ref[...]) * (1.0 + sc_ref[...]) + sh_ref[...]
    proj = _dot(hn.astype(BF), w_ref[...])
    if absorbed:
        sbq_ref[...] = (proj[:, :512] * QK_SCALE).astype(BF)
    else:
        sbq_ref[...] = _pad_heads(proj[:, :512] * QK_SCALE).astype(BF)
        o5_ref[...] = proj[:, 512:768].astype(BF)
    sbrows_ref[...] = proj[:, 512:768]
    cqn = _rms(proj[:, 768:1024], gq_ref[...])
    ckvn = _rms(proj[:, 1024:1280], gkv_ref[...])
    krr = proj[:, 1280:1408] * c32_ref[...] + proj[:, 1408:1536] * s32_ref[...]
    mla_ref[:, :256] = ckvn
    mla_ref[:, 256:288] = krr[:, :32]
    qu = _dot(cqn.astype(BF), wuq_ref[...])
    qr = (qu[:, 512:768] * c256_ref[...] + qu[:, 768:1024] * s256_ref[...]) * MLA_SCALE
    qn = qu[:, :512]
    o2_ref[...] = qr.astype(BF)
    if absorbed:
        o1_ref[...] = (_dot(qn.astype(BF), wa_ref[...]) * MLA_SCALE).astype(BF)
    else:
        o1_ref[...] = (qn * MLA_SCALE).astype(BF)
        ckb = ckvn.astype(BF)
        o3_ref[...] = _dot(ckb, wa_ref[...]).astype(BF)
        o4_ref[...] = _dot(ckb, wb_ref[...]).astype(BF)


def even_front(x2, g, sc, sh, wext, tabs, gq, gkv, wuq, wa, wb, *, tm, mod_rows, tab_rows, absorbed):
    N, D = x2.shape
    R = sc.shape[1]
    c32, s32, c256, s256 = tabs
    nt = tab_rows // tm
    row = lambda w: pl.BlockSpec((tm, w), lambda i: (i, 0))
    tab = lambda w: pl.BlockSpec((tm, w), lambda i: (i % nt, 0))
    full = lambda a: pl.BlockSpec(a.shape, lambda i: (0,) * a.ndim)
    mod = _mod_spec(tm, D, mod_rows)(R)
    in_specs = [row(D), full(g), mod, mod, full(wext), tab(128), tab(128), tab(256), tab(256), full(gq), full(gkv),
                full(wuq), full(wa)]
    args = [x2, g, sc, sh, wext, c32, s32, c256, s256, gq, gkv, wuq, wa]
    if absorbed:
        body = lambda *r: _even_front_body(*r[:13], None, *r[13:], None, None, None, absorbed=True)
        outs = [(512, BF), (256, F32), (288, F32), (2048, BF), (256, BF)]
    else:
        in_specs.append(full(wb))
        args.append(wb)
        body = functools.partial(_even_front_body, absorbed=False)
        outs = [(1024, BF), (256, F32), (288, F32), (512, BF), (256, BF), (512, BF), (512, BF), (256, BF)]
    return pl.pallas_call(
        body,
        grid=(N // tm,),
        in_specs=in_specs,
        out_specs=[row(w) for w, _ in outs],
        out_shape=[jax.ShapeDtypeStruct((N, w), dt) for w, dt in outs],
        compiler_params=_cparams(("parallel",)),
        name="even_front",
    )(*args)


def _log_sigmoid(z):
    return jnp.minimum(z, 0.0) - jnp.log(1.0 + jnp.exp(-jnp.abs(z)))


def _odd_front_body(x_ref, g_ref, sc_ref, sh_ref, w_ref, c512_ref, s512_ref, c128_ref, s128_ref, fb_ref,
                    nq_ref, nqf_ref, cmp_ref, slc_ref, win_ref, gates_ref, fq_ref, fox_ref, logf_ref, *kvb_refs, packed):
    hn = _rms(x_ref[...], g_ref[...]) * (1.0 + sc_ref[...]) + sh_ref[...]
    p = _dot(hn.astype(BF), w_ref[...])
    nq = (p[:, 0:512] * c512_ref[...] + p[:, 512:1024] * s512_ref[...]) * QK_SCALE
    nq_ref[...] = (_pad_heads(nq) if packed else nq).astype(BF)
    nqf_ref[...] = nq
    c128, s128 = c128_ref[...], s128_ref[...]
    cmp_ref[:, :128] = p[:, 1024:1152] * c128 + p[:, 1152:1280] * s128
    cmp_ref[:, 128:] = p[:, 1280:1408]
    slc_ref[:, :128] = p[:, 1408:1536] * c128 + p[:, 1536:1664] * s128
    slc_ref[:, 128:] = p[:, 1664:1792]
    win_ref[:, :128] = p[:, 1792:1920] * c128 + p[:, 1920:2048] * s128
    win_ref[:, 128:] = p[:, 2048:2176]
    gates_ref[...] = jax.nn.sigmoid(p[:, 2176:2304])
    fq = p[:, 2304:2816] * QK_SCALE
    fq_ref[...] = (_pad_heads(fq) if packed else fq).astype(BF)
    fox_ref[...] = p[:, 2816:3072]
    logf_ref[...] = _log_sigmoid(p[:, 3072:3200] + fb_ref[...])
    if packed:
        for src, dst in zip((slc_ref, win_ref, fox_ref), kvb_refs):
            dst[...] = src[...].astype(BF)


def odd_front(x2, g, sc, sh, wext, tabs, fb, *, tm, mod_rows, tab_rows, packed):
    N, D = x2.shape
    R = sc.shape[1]
    nt = tab_rows // tm
    row = lambda w: pl.BlockSpec((tm, w), lambda i: (i, 0))
    tab = lambda w: pl.BlockSpec((tm, w), lambda i: (i % nt, 0))
    full = lambda a: pl.BlockSpec(a.shape, lambda i: (0,) * a.ndim)
    mod = _mod_spec(tm, D, mod_rows)(R)
    qw = 1024 if packed else 512
    outs = [(qw, BF), (512, F32), (256, F32), (256, F32), (256, F32), (128, F32), (qw, BF), (256, F32), (128, F32)]
    if packed:
        outs += [(256, BF)] * 3
    return pl.pallas_call(
        functools.partial(_odd_front_body, packed=packed),
        grid=(N // tm,),
        in_specs=[row(D), full(g), mod, mod, full(wext), tab(512), tab(512), tab(128), tab(128), full(fb)],
        out_specs=[row(w) for w, _ in outs],
        out_shape=[jax.ShapeDtypeStruct((N, w), dt) for w, dt in outs],
        compiler_params=_cparams(("parallel",)),
        name="odd_front",
    )(x2, g, sc, sh, wext, *tabs, fb)


def _outproj_body(mix_ref, w_ref, x_ref, gt_ref, o_ref):
    o_ref[...] = x_ref[...] + gt_ref[...] * _dot(mix_ref[...].astype(BF), w_ref[...])


def out_proj_res(mix, w, x2, gt, *, tm, mod_rows):
    N, D = x2.shape
    K = mix.shape[1]
    return pl.pallas_call(
        _outproj_body,
        grid=(N // tm,),
        in_specs=[pl.BlockSpec((tm, K), lambda i: (i, 0)), pl.BlockSpec(w.shape, lambda i: (0, 0)),
                  pl.BlockSpec((tm, D), lambda i: (i, 0)), _mod_spec(tm, D, mod_rows)(gt.shape[1])],
        out_specs=pl.BlockSpec((tm, D), lambda i: (i, 0)),
        out_shape=jax.ShapeDtypeStruct((N, D), F32),
        compiler_params=_cparams(("parallel",)),
        name="out_proj_res",
    )(mix, w, x2, gt)


def _nsa_combine_body(oc_ref, os_ref, ow_ref, g_ref, e_ref, of_ref, o_ref):
    a, b, c = _split3(g_ref[...])
    e = e_ref[...]
    gx = _dot(a, e) + _dot(b, e) + _dot(c, e)
    o = (gx[:, :512] * oc_ref[...].astype(F32) + gx[:, 512:1024] * os_ref[...].astype(F32)
         + gx[:, 1024:] * ow_ref[...].astype(F32))
    o_ref[:, :512] = o.astype(BF)
    o_ref[:, 512:] = of_ref[...]


def nsa_combine(oc, osel, ow, gates, expand, of, *, tm):
    N = oc.shape[0]
    row = lambda w: pl.BlockSpec((tm, w), lambda i: (i, 0))
    return pl.pallas_call(
        _nsa_combine_body,
        grid=(N // tm,),
        in_specs=[row(512), row(512), row(512), row(128), pl.BlockSpec(expand.shape, lambda i: (0, 0)), row(512)],
        out_specs=row(1024),
        out_shape=jax.ShapeDtypeStruct((N, 1024), BF),
        compiler_params=_cparams(("parallel",)),
        name="nsa_combine",
    )(oc, osel, ow, gates, expand, of)


def _final_norm_body(x_ref, g_ref, o_ref):
    o_ref[...] = _rms(x_ref[...], g_ref[...])


def final_norm(x2, g, *, tm):
    N, D = x2.shape
    return pl.pallas_call(
        _final_norm_body,
        grid=(N // tm,),
        in_specs=[pl.BlockSpec((tm, D), lambda i: (i, 0)), pl.BlockSpec((1, D), lambda i: (0, 0))],
        out_specs=pl.BlockSpec((tm, D), lambda i: (i, 0)),
        out_shape=jax.ShapeDtypeStruct((N, D), F32),
        compiler_params=_cparams(("parallel",)),
        name="final_norm",
    )(x2, g)


def _moe_body(x_ref, g_ref, sc_ref, sh_ref, gt_ref, wr_ref, br_ref, wg_ref, wu_ref, wd_ref, o_ref, hn_s, gates_s, acc_s):
    e = pl.program_id(1)
    lane = lax.broadcasted_iota(jnp.int32, gates_s.shape, 1)
    lanef = lane.astype(F32)

    @pl.when(e == 0)
    def _():
        hn = _rms(x_ref[...], g_ref[...]) * (1.0 + sc_ref[...]) + sh_ref[...]
        hn_s[...] = hn.astype(BF)
        lr = jnp.dot(hn, wr_ref[...], precision=HI, preferred_element_type=F32) + br_ref[...]
        lg = jnp.where((lane >= MOE_EXPERTS) & (lane < MOE_EXPERTS + MOE_GROUPS), lr, -jnp.inf)
        mg = jnp.max(lg, axis=1, keepdims=True)
        g_sel = jnp.min(jnp.where(lg == mg, lanef, 1e9), axis=1, keepdims=True) - MOE_EXPERTS
        g_w = 1.0 / jnp.sum(jnp.exp(lg - mg), axis=1, keepdims=True)
        ingrp = (lane >> 3).astype(F32) == g_sel
        ingrp = ingrp & (lane < MOE_EXPERTS)
        lrm = jnp.where(ingrp, lr, -jnp.inf)
        m1 = jnp.max(lrm, axis=1, keepdims=True)
        ex = jnp.exp(lrm - m1)
        i1 = jnp.min(jnp.where(ingrp & (ex == 1.0), lanef, 1e9), axis=1, keepdims=True)
        c2 = jnp.where(ingrp & (lanef != i1), ex, -1.0)
        v2 = jnp.max(c2, axis=1, keepdims=True)
        i2 = jnp.min(jnp.where(c2 == v2, lanef, 1e9), axis=1, keepdims=True)
        den = 1.0 + v2
        gates_s[...] = jnp.where(lanef == i1, g_w / den, 0.0) + jnp.where(lanef == i2, g_w * v2 / den, 0.0)
        acc_s[...] = jnp.zeros_like(acc_s)

    hb = hn_s[...]
    gates = gates_s[...]
    hids = []
    for k in range(MOE_EXPERTS_PER_STEP):
        ge = jnp.sum(jnp.where(lane == e * MOE_EXPERTS_PER_STEP + k, gates, 0.0), axis=1, keepdims=True)
        a = _dot(hb, wg_ref[k])
        u = _dot(hb, wu_ref[k])
        hids.append((a * jax.nn.sigmoid(a) * u * ge).astype(BF))
    wd = wd_ref[...]
    acc_s[...] += _dot(jnp.concatenate(hids, axis=1), wd.reshape(wd.shape[0] * wd.shape[1], wd.shape[2]))

    @pl.when(e == pl.num_programs(1) - 1)
    def _():
        o_ref[...] = x_ref[...] + gt_ref[...] * acc_s[...]


def moe_layer(x2, g, sc, sh, gt, wr, br, wg, wu, wd, l, *, tm, mod_rows):
    N, D = x2.shape
    Hh = wg.shape[-1]
    row = pl.BlockSpec((tm, D), lambda i, e: (i, 0))
    mod = lambda a: pl.BlockSpec((None, a.shape[1], D), lambda i, e: ((i * tm) // mod_rows, 0, 0))
    full = lambda a: pl.BlockSpec(a.shape, lambda i, e: (0,) * a.ndim)
    return pl.pallas_call(
        _moe_body,
        grid=(N // tm, MOE_EXPERTS // MOE_EXPERTS_PER_STEP),
        in_specs=[row, full(g), mod(sc), mod(sh), mod(gt), full(wr), full(br),
                  pl.BlockSpec((None, MOE_EXPERTS_PER_STEP, D, Hh), lambda i, e: (l, e, 0, 0)),
                  pl.BlockSpec((None, MOE_EXPERTS_PER_STEP, D, Hh), lambda i, e: (l, e, 0, 0)),
                  pl.BlockSpec((None, MOE_EXPERTS_PER_STEP, Hh, D), lambda i, e: (l, e, 0, 0))],
        out_specs=row,
        out_shape=jax.ShapeDtypeStruct((N, D), F32),
        scratch_shapes=[pltpu.VMEM((tm, D), BF), pltpu.VMEM((tm, LANES), F32), pltpu.VMEM((tm, D), F32)],
        compiler_params=_cparams(("parallel", "arbitrary")),
        name="moe",
    )(x2, g, sc, sh, gt, wr, br, wg, wu, wd)


def _softplus(z):
    return jnp.maximum(z, 0.0) + jnp.log(1.0 + jnp.exp(-jnp.abs(z)))


def _pflash_body(*refs, mode, gb, hg, tq, tk, nk, dv, window):
    it = iter(refs)
    q_ref, k_ref, v_ref = next(it), next(it), next(it)
    cq_ref = ck_ref = sel_ref = e_ref = u_ref = None
    if mode == "fox":
        cq_ref, ck_ref = next(it), next(it)
    if mode == "sel":
        sel_ref, e_ref = next(it), next(it)
    if mode == "sb":
        u_ref = next(it)
    o_ref, m_s, acc_s = next(it), next(it), next(it)
    i, j = pl.program_id(2), pl.program_id(3)
    R = hg * tq
    last = (i * tq + tq - 1) // tk

    @pl.when(j == 0)
    def _():
        m_s[...] = jnp.full(m_s.shape, 0.0 if mode == "sb" else -jnp.inf, F32)
        acc_s[...] = jnp.zeros_like(acc_s)

    if mode == "win":
        kt = last - (nk - 1) + j
        active = kt >= 0
        edge = active
    elif mode == "sb":
        kt = last - j
        active = kt >= 0
        edge = j == 0
    else:
        kt = j
        active = j <= last
        edge = j == last

    def tile(gi, positional):
        q = q_ref[gi].reshape(R, q_ref.shape[-1])
        s3 = _dot_t(q, k_ref[gi]).reshape(hg, tq, tk)
        mask = None
        if positional:
            qpos = i * tq + lax.broadcasted_iota(jnp.int32, (tq, tk), 0)
            kpos = kt * tk + lax.broadcasted_iota(jnp.int32, (tq, tk), 1)
            if mode == "sb":
                mask = kpos < qpos
            elif mode == "win":
                d = qpos - kpos
                mask = (d >= 0) & (d < window)
            else:
                mask = kpos <= qpos
        if mode == "sel":
            picked = _dot(sel_ref[gi], e_ref[...]) > 0.5
            mask = picked if mask is None else mask & picked
        if mode == "fox":
            s3 = s3 - ck_ref[gi]
        chunk = lambda a: [a[:, c * LANES:(c + 1) * LANES] for c in range(tk // LANES)]
        if mode == "sb":
            lk3 = -_softplus(s3)
            if mask is not None:
                lk3 = jnp.where(mask[None], lk3, 0.0)
            lk = lk3.reshape(R, tk)
            hi = lk.astype(BF)
            lo = (lk - hi.astype(F32)).astype(BF)
            u = u_ref[...]
            sufx = _dot(hi, u) + _dot(lo, u)
            carry = m_s[gi]
            e = jnp.concatenate([a + b + carry for a, b in zip(chunk((s3 + lk3).reshape(R, tk)), chunk(sufx))], axis=1)
            w3 = jnp.exp(e.reshape(hg, tq, tk))
            if mask is not None:
                w3 = jnp.where(mask[None], w3, 0.0)
            acc_s[gi] += _dot(w3.reshape(R, tk).astype(BF), v_ref[gi])
            m_s[gi] = carry + sufx[:, tk:]
        else:
            if mask is not None:
                s3 = jnp.where(mask[None], s3, -jnp.inf)
            chunks = chunk(s3.reshape(R, tk))
            if mode == "fox":
                cq = cq_ref[gi].reshape(R, LANES)
                chunks = [a + cq for a in chunks]
            m_prev = m_s[gi]
            m_new = jnp.maximum(m_prev, jnp.max(functools.reduce(jnp.maximum, chunks), axis=1, keepdims=True))
            m_use = jnp.where(m_new > -jnp.inf, m_new, 0.0)
            p = jnp.concatenate([jnp.exp(a - m_use) for a in chunks], axis=1).astype(BF)
            acc_s[gi] = jnp.exp(m_prev - m_use) * acc_s[gi] + _dot(p, v_ref[gi])
            m_s[gi] = m_new

    @pl.when(active & edge)
    def _():
        for gi in range(gb):
            tile(gi, True)

    if mode != "win":
        @pl.when(active & jnp.logical_not(edge))
        def _():
            for gi in range(gb):
                tile(gi, False)

    @pl.when(j == nk - 1)
    def _():
        acc = acc_s[...]
        out = acc[:, :, :dv]
        if mode != "sb":
            l = acc[:, :, dv:dv + 1]
            out = out / jnp.where(l > 0.0, l, 1.0)
        o_ref[...] = out.reshape(o_ref.shape).astype(o_ref.dtype)


def _ones_column(v):
    pad = LANES - v.shape[-1] - 1
    return jnp.concatenate([v, jnp.ones(v.shape[:-1] + (1,), v.dtype), jnp.zeros(v.shape[:-1] + (pad,), v.dtype)], axis=-1)


def pflash(q, k, v, *, mode, tq, tk, gb=2, extras=(), window=NSA_WINDOW):
    B, G, hg, T, dk = q.shape
    dv = v.shape[-1]
    tq, tk = _pick_tile(T, tq), _pick_tile(T, tk)
    assert tk % LANES == 0 and tk % tq == 0
    gb = _pick_tile(G, gb)
    nq = T // tq
    last = lambda i: (i * tq + tq - 1) // tk
    if mode == "win":
        nk = max(last(i) - max(i * tq - window + 1, 0) // tk + 1 for i in range(nq))
        kidx = lambda i, j: jnp.maximum(last(i) - (nk - 1) + j, 0)
    elif mode == "sb":
        nk = T // tk
        kidx = lambda i, j: jnp.maximum(last(i) - j, 0)
    else:
        nk = T // tk
        kidx = lambda i, j: jnp.minimum(j, last(i))
    in_specs = [pl.BlockSpec((None, gb, hg, tq, dk), lambda b, g, i, j: (b, g, 0, i, 0)),
                pl.BlockSpec((None, gb, tk, dk), lambda b, g, i, j: (b, g, kidx(i, j), 0)),
                pl.BlockSpec((None, gb, tk, LANES), lambda b, g, i, j: (b, g, kidx(i, j), 0))]
    if mode == "fox":
        in_specs += [pl.BlockSpec((None, gb, hg, tq, LANES), lambda b, g, i, j: (b, g, 0, i, 0)),
                     pl.BlockSpec((None, gb, hg, 1, tk), lambda b, g, i, j: (b, g, 0, 0, kidx(i, j)))]
    if mode == "sel":
        nbp = extras[0].shape[-1]
        in_specs += [pl.BlockSpec((None, gb, tq, nbp), lambda b, g, i, j: (b, g, i, 0)),
                     pl.BlockSpec((nbp, tk), lambda b, g, i, j: (0, kidx(i, j)))]
    if mode == "sb":
        extras = (jnp.asarray(np.concatenate([np.tril(np.ones((tk, tk), np.float32), -1), np.ones((tk, LANES), np.float32)],
                                             axis=1), dtype=BF),)
        in_specs += [pl.BlockSpec((tk, tk + LANES), lambda b, g, i, j: (0, 0))]
    R = hg * tq
    return pl.pallas_call(
        functools.partial(_pflash_body, mode=mode, gb=gb, hg=hg, tq=tq, tk=tk, nk=nk, dv=dv, window=window),
        grid=(B, G // gb, nq, nk),
        in_specs=in_specs,
        out_specs=pl.BlockSpec((None, gb, hg, tq, dv), lambda b, g, i, j: (b, g, 0, i, 0)),
        out_shape=jax.ShapeDtypeStruct((B, G, hg, T, dv), BF),
        scratch_shapes=[pltpu.VMEM((gb, R, LANES), F32), pltpu.VMEM((gb, R, LANES), F32)],
        compiler_params=_cparams(("parallel", "parallel", "parallel", "arbitrary")),
        name="pflash_" + mode,
    )(q, k, _ones_column(v), *extras)


def _pad_heads(q):
    low = lax.broadcasted_iota(jnp.int32, (q.shape[0], LANES), 1) < HEAD_DIM
    tiles = []
    for h in range(8):
        t = q[:, (h // 2) * LANES:(h // 2 + 1) * LANES]
        if h % 2 != h // 4:
            t = pltpu.roll(t, HEAD_DIM, 1)
        tiles.append(jnp.where(low if h < 4 else jnp.logical_not(low), t, 0.0))
    return jnp.concatenate(tiles, axis=1)


def _gflash_body(*refs, mode, hg, tq, tk, nk, window):
    it = iter(refs)
    q_ref, k_ref, v_ref = next(it), next(it), next(it)
    cq_ref = ck_ref = sel_ref = e_ref = u_ref = None
    if mode == "fox":
        cq_ref, ck_ref = next(it), next(it)
    if mode == "sel":
        sel_ref, e_ref = next(it), next(it)
    if mode == "sb":
        u_ref = next(it)
    o_ref, m_s, l_s, acc_s = next(it), next(it), next(it), next(it)
    i, j = pl.program_id(1), pl.program_id(2)
    R = hg * tq
    last = (i * tq + tq - 1) // tk

    @pl.when(j == 0)
    def _():
        m_s[...] = jnp.full(m_s.shape, 0.0 if mode == "sb" else -jnp.inf, F32)
        l_s[...] = jnp.zeros_like(l_s)
        acc_s[...] = jnp.zeros_like(acc_s)

    if mode == "win":
        kt = last - (nk - 1) + j
        active = kt >= 0
        edge = active
    elif mode == "sb":
        kt = last - j
        active = kt >= 0
        edge = j == 0
    else:
        kt = j
        active = j <= last
        edge = j == last

    def tile(g, positional):
        q = jnp.concatenate([q_ref[:, (g * hg + h) * LANES:(g * hg + h + 1) * LANES] for h in range(hg)], axis=0)
        s3 = _dot_t(q, k_ref[...]).reshape(hg, tq, tk)
        mask = None
        if positional:
            qpos = i * tq + lax.broadcasted_iota(jnp.int32, (tq, tk), 0)
            kpos = kt * tk + lax.broadcasted_iota(jnp.int32, (tq, tk), 1)
            if mode == "sb":
                mask = kpos < qpos
            elif mode == "win":
                d = qpos - kpos
                mask = (d >= 0) & (d < window)
            else:
                mask = kpos <= qpos
        if mode == "sel":
            picked = _dot(sel_ref[g], e_ref[...]) > 0.5
            mask = picked if mask is None else mask & picked
        if mode == "fox":
            s3 = s3 - ck_ref[g]
        chunk = lambda a: [a[:, c * LANES:(c + 1) * LANES] for c in range(tk // LANES)]
        if mode == "sb":
            lk3 = -_softplus(s3)
            if mask is not None:
                lk3 = jnp.where(mask[None], lk3, 0.0)
            sufx = _dot(lk3.reshape(R, tk).astype(BF), u_ref[...])
            carry = m_s[g]
            e = jnp.concatenate([a + b + carry for a, b in zip(chunk((s3 + lk3).reshape(R, tk)), chunk(sufx))], axis=1)
            w3 = jnp.exp(e.reshape(hg, tq, tk))
            if mask is not None:
                w3 = jnp.where(mask[None], w3, 0.0)
            acc_s[g] += _dot(w3.reshape(R, tk).astype(BF), v_ref[...])
            m_s[g] = carry + sufx[:, tk:]
        else:
            if mask is not None:
                s3 = jnp.where(mask[None], s3, -jnp.inf)
            chunks = chunk(s3.reshape(R, tk))
            if mode == "fox":
                cq = cq_ref[g].reshape(R, LANES)
                chunks = [a + cq for a in chunks]
            m_prev = m_s[g]
            m_new = jnp.maximum(m_prev, jnp.max(functools.reduce(jnp.maximum, chunks), axis=1, keepdims=True))
            m_use = jnp.where(m_new > -jnp.inf, m_new, 0.0)
            p = jnp.concatenate([jnp.exp(a - m_use) for a in chunks], axis=1).astype(BF)
            alpha = jnp.exp(m_prev - m_use)
            l_s[g] = alpha * l_s[g] + _dot(p, jnp.ones((tk, LANES), BF))
            acc_s[g] = alpha * acc_s[g] + _dot(p, v_ref[...])
            m_s[g] = m_new

    @pl.when(active & edge)
    def _():
        for g in range(2):
            tile(g, True)

    if mode != "win":
        @pl.when(active & jnp.logical_not(edge))
        def _():
            for g in range(2):
                tile(g, False)

    @pl.when(j == nk - 1)
    def _():
        for g in range(2):
            acc = acc_s[g]
            if mode != "sb":
                l = l_s[g]
                acc = acc / jnp.where(l > 0.0, l, 1.0)
            heads = [acc[h * tq:(h + 1) * tq, g * HEAD_DIM:(g + 1) * HEAD_DIM] for h in range(hg)]
            o_ref[:, g * hg * HEAD_DIM:(g + 1) * hg * HEAD_DIM] = jnp.concatenate(heads, axis=1).astype(o_ref.dtype)


def gflash(qp, kv, *, mode, tq, tk, extras=(), window=NSA_WINDOW):
    B, T, _ = qp.shape
    hg = 4
    tq, tk = _pick_tile(T, tq), _pick_tile(T, tk)
    assert tk % LANES == 0 and tk % tq == 0
    nq = T // tq
    last = lambda i: (i * tq + tq - 1) // tk
    if mode == "win":
        nk = max(last(i) - max(i * tq - window + 1, 0) // tk + 1 for i in range(nq))
        kidx = lambda i, j: jnp.maximum(last(i) - (nk - 1) + j, 0)
    elif mode == "sb":
        nk = T // tk
        kidx = lambda i, j: jnp.maximum(last(i) - j, 0)
    else:
        nk = T // tk
        kidx = lambda i, j: jnp.minimum(j, last(i))
    in_specs = [pl.BlockSpec((None, tq, 8 * LANES), lambda b, i, j: (b, i, 0)),
                pl.BlockSpec((None, tk, LANES), lambda b, i, j: (b, kidx(i, j), 0)),
                pl.BlockSpec((None, tk, LANES), lambda b, i, j: (b, kidx(i, j), 1))]
    if mode == "fox":
        in_specs += [pl.BlockSpec((None, 2, hg, tq, LANES), lambda b, i, j: (b, 0, 0, i, 0)),
                     pl.BlockSpec((None, 2, hg, 1, tk), lambda b, i, j: (b, 0, 0, 0, kidx(i, j)))]
    if mode == "sel":
        nbp = extras[0].shape[-1]
        in_specs += [pl.BlockSpec((None, 2, tq, nbp), lambda b, i, j: (b, 0, i, 0)),
                     pl.BlockSpec((nbp, tk), lambda b, i, j: (0, kidx(i, j)))]
    if mode == "sb":
        extras = (jnp.asarray(np.concatenate([np.tril(np.ones((tk, tk), np.float32), -1), np.ones((tk, LANES), np.float32)],
                                             axis=1), dtype=BF),)
        in_specs += [pl.BlockSpec((tk, tk + LANES), lambda b, i, j: (0, 0))]
    R = hg * tq
    return pl.pallas_call(
        functools.partial(_gflash_body, mode=mode, hg=hg, tq=tq, tk=tk, nk=nk, window=window),
        grid=(B, nq, nk),
        in_specs=in_specs,
        out_specs=pl.BlockSpec((None, tq, 2 * hg * HEAD_DIM), lambda b, i, j: (b, i, 0)),
        out_shape=jax.ShapeDtypeStruct((B, T, 2 * hg * HEAD_DIM), BF),
        scratch_shapes=[pltpu.VMEM((2, R, LANES), F32), pltpu.VMEM((2, R, LANES), F32), pltpu.VMEM((2, R, LANES), F32)],
        compiler_params=_cparams(("parallel", "parallel", "arbitrary")),
        name="gflash_" + mode,
    )(qp, kv, kv, *extras)


def _cmpsel_body(q_ref, kc_ref, vc_ref, oc_ref, sel_ref, *, hg, tq, nbc, nbs, n_sel, pos0):
    for bi in range(q_ref.shape[0]):
        for g in range(q_ref.shape[1]):
            _cmpsel_chain(q_ref.at[bi, g], kc_ref.at[bi, g], vc_ref.at[bi, g], oc_ref.at[bi, g], sel_ref.at[bi, g],
                          hg=hg, tq=tq, nbc=nbc, nbs=nbs, n_sel=n_sel, pos0=pos0)


def _cmpsel_chain(q_ref, kc_ref, vc_ref, oc_ref, sel_ref, *, hg, tq, nbc, nbs, n_sel, pos0):
    qi = pl.program_id(1)
    R = hg * tq
    nbp = kc_ref.shape[0]
    s = _dot_t(q_ref[...], kc_ref[...], precision=HI)
    n_r = lax.broadcasted_iota(jnp.int32, (R, nbp), 1)
    r_r = lax.broadcasted_iota(jnp.int32, (R, nbp), 0)
    qpos_r = pos0 + qi * tq + (r_r % tq)
    cmask = (n_r * NSA_BLOCK + (NSA_BLOCK - 1) <= qpos_r) & (n_r < nbc)
    s = jnp.where(cmask, s, -jnp.inf)
    m = jnp.max(s, axis=1, keepdims=True)
    m = jnp.where(m > -jnp.inf, m, 0.0)
    ex = jnp.exp(s - m)
    den = jnp.sum(ex, axis=1, keepdims=True)
    p = ex / jnp.where(den > 0.0, den, 1.0)
    oc_ref[...] = _dot(p.astype(BF), vc_ref[...].astype(BF)).astype(oc_ref.dtype)
    tt = lax.broadcasted_iota(jnp.int32, (tq, R), 0)
    rr = lax.broadcasted_iota(jnp.int32, (tq, R), 1)
    fold = (rr % tq == tt).astype(BF)
    a, b, c = _split3(p)
    imp = _dot(fold, a) + _dot(fold, b) + _dot(fold, c)
    n = lax.broadcasted_iota(jnp.int32, (tq, nbp), 1)
    qpos = pos0 + qi * tq + lax.broadcasted_iota(jnp.int32, (tq, nbp), 0)
    cur = qpos // NSA_BLOCK
    forced = (n == 0) | (n == cur) | (n == cur - 1)
    score = jnp.where(forced, float(hg + 1), jnp.where(n <= cur, imp, -1.0))
    score = jnp.where(n < nbs, score, -2.0)
    nf = n.astype(F32)
    sel = jnp.zeros((tq, nbp), F32)
    for _ in range(n_sel):
        mx = jnp.max(score, axis=1, keepdims=True)
        idx = jnp.min(jnp.where(score == mx, nf, 1e9), axis=1, keepdims=True)
        hit = nf == idx
        sel = jnp.where(hit & (mx >= 0.0), 1.0, sel)
        score = jnp.where(hit, -3.0, score)
    sel_ref[...] = sel.astype(sel_ref.dtype)


def nsa_cmp_sel(qf, kc, vc, *, tq, nbc, nbs, pos0, bb):
    B, G, nqt, R, d = qf.shape
    hg = R // tq
    nbp = kc.shape[2]
    n_sel = min(NSA_TOPN, nbs)
    bb = _pick_tile(B, bb)
    return pl.pallas_call(
        functools.partial(_cmpsel_body, hg=hg, tq=tq, nbc=nbc, nbs=nbs, n_sel=n_sel, pos0=pos0),
        grid=(B // bb, nqt),
        in_specs=[pl.BlockSpec((bb, G, None, R, d), lambda b, i: (b, 0, i, 0, 0)),
                  pl.BlockSpec((bb, G, nbp, d), lambda b, i: (b, 0, 0, 0)),
                  pl.BlockSpec((bb, G, nbp, d), lambda b, i: (b, 0, 0, 0))],
        out_specs=[pl.BlockSpec((bb, G, None, R, d), lambda b, i: (b, 0, i, 0, 0)),
                   pl.BlockSpec((bb, G, None, tq, nbp), lambda b, i: (b, 0, i, 0, 0))],
        out_shape=[jax.ShapeDtypeStruct((B, G, nqt, R, d), BF), jax.ShapeDtypeStruct((B, G, nqt, tq, nbp), BF)],
        compiler_params=_cparams(("parallel", "parallel")),
        name="nsa_cmp_sel",
    )(qf, kc, vc)


PAGE = 128
DEC_PAGES_PER_STEP = 16
DEC_SEQS_PER_STEP = 4
NEW_PAD = PAGE


def _dec_softmax_step(s, vs, m_s, l_s, acc_s):
    m_prev = m_s[:, :1]
    m_new = jnp.maximum(m_prev, jnp.max(s, axis=1, keepdims=True))
    m_use = jnp.where(m_new > -jnp.inf, m_new, 0.0)
    p = jnp.exp(s - m_use)
    alpha = jnp.exp(m_prev - m_use)
    l_s[...] = jnp.broadcast_to(alpha * l_s[:, :1] + jnp.sum(p, axis=1, keepdims=True), l_s.shape)
    w = vs[0].shape[1]
    pv = _dot_t(p[:, :w].astype(BF), vs[0])
    for n in range(1, len(vs)):
        pv = pv + _dot_t(p[:, n * w:(n + 1) * w].astype(BF), vs[n])
    acc_s[...] = alpha * acc_s[...] + pv
    m_s[...] = jnp.broadcast_to(m_new, m_s.shape)


def _dec_body(pt_ref, *refs, mode, G, bb, nc, n_new, rows_per_t, t_period, wlen, window):
    del pt_ref
    it = iter(refs)
    q_ref = next(it)
    cq_ref = ck_ref = cknew_ref = selx_ref = e2_ref = selnew_ref = u_ref = None
    if mode == "fox":
        cq_ref, ck_ref, cknew_ref = next(it), next(it), next(it)
    if mode == "sel":
        selx_ref, e2_ref, selnew_ref = next(it), next(it), next(it)
    if mode == "sb":
        u_ref = next(it)
    pages = [[next(it) for _ in range(G)] for _ in range(bb)]
    new_ref = next(it)
    o_ref, m_s, l_s, acc_s = next(it), next(it), next(it), next(it)
    c = pl.program_id(1)
    R = q_ref.shape[1]

    def row_t(shape):
        r = lax.broadcasted_iota(jnp.int32, shape, 0)
        return (r % t_period) // rows_per_t

    def split_kv(x):
        if mode == "mla":
            kb = x.astype(BF)
            return kb, kb[:256]
        return x[:128].astype(BF), x[128:].astype(BF)

    def new_rows(bi):
        q = q_ref[bi]
        kb, vb = split_kv(new_ref[bi])
        s = _dot(q, kb)
        t = row_t(s.shape)
        key = lax.broadcasted_iota(jnp.int32, s.shape, 1)
        if mode == "sb":
            mask = (key < t) & (key < n_new)
            lk = jnp.where(mask, -_softplus(s), 0.0)
            hi = lk.astype(BF)
            lo = (lk - hi.astype(F32)).astype(BF)
            suf = _dot(hi, u_ref[...]) + _dot(lo, u_ref[...])
            w = jnp.where(mask, jnp.exp(s + lk + suf), 0.0)
            acc_s[bi] += _dot_t(w.astype(BF), vb)
            m_s[bi] = jnp.broadcast_to(jnp.sum(lk, axis=1, keepdims=True), (R, LANES))
        else:
            mask = (key <= t) & (key < n_new)
            if mode == "sel":
                mask = mask & (selnew_ref[bi] > 0.5)
            if mode == "fox":
                s = s + cq_ref[bi][:, :1] - cknew_ref[bi]
            _dec_softmax_step(jnp.where(mask, s, -jnp.inf), [vb], m_s.at[bi], l_s.at[bi], acc_s.at[bi])

    def past_pages(bi):
        q = q_ref[bi]
        kvs = [split_kv(p[...]) for p in pages[bi]]
        zs = [_dot(q, kb) for kb, _ in kvs]
        if mode == "sb":
            u = u_ref[...]
            carry = m_s[bi][:, :1]
            pv = None
            for z, (_, vb) in zip(zs, kvs):
                lk = -_softplus(z)
                hi = lk.astype(BF)
                lo = (lk - hi.astype(F32)).astype(BF)
                suf = _dot(hi, u) + _dot(lo, u)
                w = jnp.exp(z + lk + suf + carry)
                d = _dot_t(w.astype(BF), vb)
                pv = d if pv is None else pv + d
                carry = carry + jnp.sum(lk, axis=1, keepdims=True)
            acc_s[bi] += pv
            m_s[bi] = jnp.broadcast_to(carry, (R, LANES))
        else:
            s = jnp.concatenate(zs, axis=1) if G > 1 else zs[0]
            if mode == "fox":
                s = s + cq_ref[bi][:, :1] - ck_ref[bi]
            if mode == "sel":
                s = jnp.where(_dot(selx_ref[bi], e2_ref[...]) > 0.5, s, -jnp.inf)
            if mode == "win":
                widx = c * (G * PAGE) + lax.broadcasted_iota(jnp.int32, s.shape, 1)
                s = jnp.where(widx > row_t(s.shape) + (wlen - window), s, -jnp.inf)
            _dec_softmax_step(s, [vb for _, vb in kvs], m_s.at[bi], l_s.at[bi], acc_s.at[bi])

    @pl.when(c == 0)
    def _():
        m_s[...] = jnp.full(m_s.shape, 0.0 if mode == "sb" else -jnp.inf, F32)
        l_s[...] = jnp.zeros_like(l_s)
        acc_s[...] = jnp.zeros_like(acc_s)
        for bi in range(bb):
            new_rows(bi)

    for bi in range(bb):
        past_pages(bi)

    @pl.when(c == nc - 1)
    def _():
        if mode == "sb":
            o_ref[...] = acc_s[...]
        else:
            l = l_s[...][:, :, :1]
            o_ref[...] = acc_s[...] / jnp.where(l > 0.0, l, 1.0)


def dec_attn(pt, q, cache, new, *, mode, extras=(), n_new, rows_per_t, t_period, window=NSA_WINDOW):
    B, n_pages = pt.shape
    _, R, Dq = q.shape
    W = cache.shape[1]
    G = _pick_tile(n_pages, DEC_PAGES_PER_STEP)
    bb = _pick_tile(B, DEC_SEQS_PER_STEP)
    nc = n_pages // G
    dv = 256 if mode == "mla" else 128
    if mode == "sb":
        page_idx = lambda b, c, pt, bi, p: (pt[b * bb + bi, n_pages - 1 - (c * G + p)], 0, 0)
    elif mode == "win":
        page_idx = lambda b, c, pt, bi, p: (b * bb + bi, 0, c * G + p)
    else:
        page_idx = lambda b, c, pt, bi, p: (pt[b * bb + bi, c * G + p], 0, 0)
    per_b = lambda shape: pl.BlockSpec((bb,) + shape, lambda b, c, pt: (b,) + (0,) * len(shape))
    in_specs = [per_b((R, Dq))]
    if mode == "fox":
        in_specs += [per_b((R, LANES)), pl.BlockSpec((bb, R, G * PAGE), lambda b, c, pt: (b, 0, c)), per_b((R, NEW_PAD))]
    if mode == "sel":
        in_specs += [pl.BlockSpec((bb, None, R, 2 * G), lambda b, c, pt: (b, c, 0, 0)),
                     pl.BlockSpec((2 * G, G * PAGE), lambda b, c, pt: (0, 0)), per_b((R, NEW_PAD))]
    if mode == "sb":
        in_specs += [pl.BlockSpec((PAGE, PAGE), lambda b, c, pt: (0, 0))]
    in_specs += [pl.BlockSpec((None, W, PAGE), functools.partial(page_idx, bi=bi, p=p)) for bi in range(bb) for p in range(G)]
    in_specs += [per_b((W, NEW_PAD))]
    return pl.pallas_call(
        functools.partial(_dec_body, mode=mode, G=G, bb=bb, nc=nc, n_new=n_new, rows_per_t=rows_per_t, t_period=t_period,
                          wlen=n_pages * PAGE, window=window),
        grid_spec=pltpu.PrefetchScalarGridSpec(
            num_scalar_prefetch=1,
            grid=(B // bb, nc),
            in_specs=in_specs,
            out_specs=per_b((R, dv)),
            scratch_shapes=[pltpu.VMEM((bb, R, LANES), F32), pltpu.VMEM((bb, R, LANES), F32), pltpu.VMEM((bb, R, dv), F32)],
        ),
        out_shape=jax.ShapeDtypeStruct((B, R, dv), F32),
        compiler_params=_cparams(("parallel", "arbitrary")),
        name="dec_" + mode,
    )(pt, q, *extras, *([cache] * (bb * G)), new)


def _page_rows_body(pt_ref, *refs, G):
    del pt_ref
    o_ref = refs[G]
    o_ref[...] = jnp.concatenate([r[...] for r in refs[:G]], axis=0)


def gather_page_rows(pt, cache_rows):
    B, n_pages = pt.shape
    _, H, W = cache_rows.shape
    G = _pick_tile(n_pages, 32)
    return pl.pallas_call(
        functools.partial(_page_rows_body, G=G),
        grid_spec=pltpu.PrefetchScalarGridSpec(
            num_scalar_prefetch=1,
            grid=(B, n_pages // G),
            in_specs=[pl.BlockSpec((None, H, W), lambda b, c, pt, p=p: (pt[b, c * G + p], 0, 0)) for p in range(G)],
            out_specs=pl.BlockSpec((None, G * H, W), lambda b, c, pt: (b, c, 0)),
        ),
        out_shape=jax.ShapeDtypeStruct((B, n_pages * H, W), cache_rows.dtype),
        compiler_params=_cparams(("parallel", "parallel")),
        name="gather_page_rows",
    )(pt, *([cache_rows] * G))


def _cmp_pages_body(pt_ref, *refs, G, nc, n_pages):
    del pt_ref
    pages = refs[:G]
    w_ref, b_ref, o_ref, x_s = refs[G:]
    c = pl.program_id(1)
    F = pages[0].shape[0]
    for p in range(G):
        x_s[pl.ds(pl.multiple_of((c * G + p) * F, F), F), :] = pages[p][...]

    @pl.when(c == nc - 1)
    def _():
        for kv in range(2):
            for g in range(NSA_GROUPS):
                f0 = (kv * NSA_GROUPS + g) * HEAD_DIM
                x = jnp.concatenate([x_s[pl.ds(f0 + d, n_pages, stride=F), :] for d in range(HEAD_DIM)], axis=1)
                col = (kv * NSA_GROUPS + g) * LANES
                o_ref[:, col:col + LANES] = _dot(x.astype(BF), w_ref[kv]) + b_ref[:, col:col + LANES]


def compress_pages(pt, cache_t, w2, bias):
    B, n_pages = pt.shape
    F = cache_t.shape[1]
    G = _pick_tile(n_pages, 16)
    nc = n_pages // G
    return pl.pallas_call(
        functools.partial(_cmp_pages_body, G=G, nc=nc, n_pages=n_pages),
        grid_spec=pltpu.PrefetchScalarGridSpec(
            num_scalar_prefetch=1,
            grid=(B, nc),
            in_specs=[pl.BlockSpec((None, F, PAGE), lambda b, c, pt, p=p: (pt[b, c * G + p], 0, 0)) for p in range(G)]
            + [pl.BlockSpec(w2.shape, lambda b, c, pt: (0, 0, 0)), pl.BlockSpec(bias.shape, lambda b, c, pt: (0, 0))],
            out_specs=pl.BlockSpec((None, n_pages, 4 * LANES), lambda b, c, pt: (b, 0, 0)),
            scratch_shapes=[pltpu.VMEM((n_pages * F, PAGE), F32)],
        ),
        out_shape=jax.ShapeDtypeStruct((B, n_pages, 4 * LANES), F32),
        compiler_params=_cparams(("parallel", "arbitrary")),
        name="compress_pages",
    )(pt, *([cache_t] * G), w2, bias)


def _page_prefix_body(within_ref, tot_ref, l_ref, o_ref):
    a, b, c = _split3(tot_ref[...])
    l = l_ref[...]
    o_ref[...] = within_ref[...] + _dot(l, a) + _dot(l, b) + _dot(l, c)


def page_prefix(y, lstrict):
    B, npg, W2 = y.shape
    W = W2 // 2
    return pl.pallas_call(
        _page_prefix_body,
        grid=(B,),
        in_specs=[pl.BlockSpec((None, npg, W), lambda b: (b, 0, 0)), pl.BlockSpec((None, npg, W), lambda b: (b, 0, 1)),
                  pl.BlockSpec((npg, npg), lambda b: (0, 0))],
        out_specs=pl.BlockSpec((None, npg, W), lambda b: (b, 0, 0)),
        out_shape=jax.ShapeDtypeStruct((B, npg, W), F32),
        compiler_params=_cparams(("parallel",)),
        name="page_prefix",
    )(y, y, lstrict)


def _swap_halves(w, hd):
    K, N = w.shape
    return w.reshape(K, N // hd, 2, hd // 2)[:, :, ::-1, :].reshape(K, N)


def _pad_last(a, n):
    return jnp.pad(a, [(0, 0)] * (a.ndim - 1) + [(0, n - a.shape[-1])])


def _rope_tabs(pos, d):
    inv = ROPE_THETA ** (-jnp.arange(0, d, 2, dtype=F32) / d)
    ang = pos.astype(F32)[:, None] * inv[None, :]
    c, s = jnp.cos(ang), jnp.sin(ang)
    return jnp.concatenate([c, c], axis=1), jnp.concatenate([-s, s], axis=1)


def _prep_weights(W):
    D = W["ev_w_in"].shape[1]
    P = {}
    w = W["ev_w_in"][0]
    kr = w[:, 1280:1312]
    P["ev_wext"] = jnp.concatenate([w[:, :1280], _pad_last(kr, 128), _pad_last(_swap_halves(kr, 32), 128)], axis=1).astype(BF)
    wuq = W["ev_w_uq"][0]
    wn, wr = wuq[:, :, :MLA_NOPE].reshape(256, 512), wuq[:, :, MLA_NOPE:].reshape(256, 256)
    P["ev_wuq"] = jnp.concatenate([wn, wr, _swap_halves(wr, MLA_ROPE)], axis=1).astype(BF)
    wuk, wuv = W["ev_w_uk"][0], W["ev_w_uv"][0]
    eye8 = jnp.eye(MLA_HEADS, dtype=F32)
    P["ev_wuk"] = wuk.reshape(256, 512).astype(BF)
    P["ev_wuv"] = wuv.reshape(256, 512).astype(BF)
    P["ev_wabs"] = jnp.einsum("chn,hH->hnHc", wuk, eye8).reshape(512, 2048).astype(BF)
    P["ev_wuv_bd"] = jnp.einsum("chv,hH->hcHv", wuv, eye8).reshape(2048, 512).astype(BF)
    P["ev_wout"] = W["ev_w_out"][0].astype(BF)
    P["ev_gq"] = W["ev_q_norm"][0][None, :]
    P["ev_gkv"] = W["ev_kv_norm"][0][None, :]

    w = W["od_w_in"][0]
    seg = lambda a, b: w[:, a:b]
    nq, ck, cv, sk, sv, wk, wv = seg(0, 512), seg(512, 640), seg(640, 768), seg(768, 896), seg(896, 1024), seg(1024, 1152), seg(1152, 1280)
    gl, fq, fk, fv, fl = seg(1280, 1304), seg(1304, 1816), seg(1816, 1944), seg(1944, 2072), seg(2072, 2080)
    sw = lambda a: _swap_halves(a, HEAD_DIM)
    P["od_wext"] = jnp.concatenate([nq, sw(nq), ck, sw(ck), cv, sk, sw(sk), sv, wk, sw(wk), wv, _pad_last(gl, 128), fq, fk, fv,
                                    _pad_last(fl, 128)], axis=1).astype(BF)
    P["od_fb"] = _pad_last(W["od_fox_fb"][0][None, :], 128)
    P["od_wout"] = W["od_w_out"][0].astype(BF)
    eye2 = jnp.eye(2, dtype=F32)
    wbig = jnp.einsum("klde,kK,gG->lkgdKGe", W["od_cmp_w"][0], eye2, eye2).reshape(NSA_BLOCK * 256, 256)
    pe_flat = jnp.broadcast_to(W["od_cmp_pe"][0].transpose(1, 0, 2)[:, :, None, :], (NSA_BLOCK, 2, 2, HEAD_DIM)).reshape(1, -1)
    P["cmp_w"] = wbig.astype(BF)
    P["cmp_b"] = linear(jnp.pad(pe_flat, ((0, 7), (0, 0))), wbig, mode="hi", tn=128, name="cmp_bias")[:1]
    P["cmp_w2"] = jnp.einsum("klde,hH->kdhlHe", W["od_cmp_w"][0], eye2).reshape(2, HEAD_DIM * PAGE, 2 * HEAD_DIM).astype(BF)
    P["cmp_b2"] = jnp.broadcast_to(P["cmp_b"].reshape(2, 2, 1, HEAD_DIM), (2, 2, 2, HEAD_DIM)).reshape(1, 8 * HEAD_DIM)
    r = np.arange(128)
    cidx = np.arange(1536)
    P["gate_expand"] = jnp.asarray(((r[:, None] // 3 == (cidx[None, :] % 512) // 64) & (r[:, None] % 3 == cidx[None, :] // 512)
                                    & (r[:, None] < 24)).astype(np.float32), dtype=BF)
    i = np.arange(PAGE)
    P["scan_w"] = jnp.asarray(np.concatenate([i[:, None] <= i[None, :], np.ones((PAGE, PAGE), bool)], axis=1).astype(np.float32),
                              dtype=BF)
    P["moe_wg"] = W["moe_w_gate"].astype(BF)
    P["moe_wu"] = W["moe_w_up"].astype(BF)
    P["moe_wd"] = W["moe_w_down"].astype(BF)
    return P


def _mods(c, W, l, T, per_token):
    B, D = c.shape
    m = linear(c, W["ada_w"], W["ada_b"].reshape(W["ada_b"].shape[0], 1, -1), widx=l, act="silu", name="ada_mod").reshape(B, 6, D)
    if per_token:
        return [jnp.repeat(m[:, j], T, axis=0)[None] for j in range(6)]
    return [m[:, j][:, None, :] for j in range(6)]


def _heads_major(a, B, T, G, hg):
    return a.reshape(B, T, G, hg, HEAD_DIM).transpose(0, 2, 3, 1, 4)


def _kv_major(a, B, T, G):
    return a.reshape(B, T, G, HEAD_DIM).transpose(0, 2, 1, 3)


def _from_heads_major(o, N):
    B, G, hg, T, d = o.shape
    return o.transpose(0, 3, 1, 2, 4).reshape(N, G * hg * d)


def _block_diag_q(q, B, T):
    q5 = q.reshape(B, T, 2, 4, HEAD_DIM).transpose(0, 2, 1, 3, 4).reshape(B, 2, T * 4, 1, HEAD_DIM)
    eye = jnp.eye(2, dtype=q.dtype)[None, :, None, :, None]
    return (q5 * eye).reshape(B, 2 * T * 4, 2 * HEAD_DIM)


def _pick_diag(o, B, T):
    o6 = o.reshape(B, 2, T, 4, 2, HEAD_DIM)
    return jnp.stack([o6[:, 0, :, :, 0], o6[:, 1, :, :, 1]], axis=2).reshape(B * T, 512)


def _pad_new(rows, B, T):
    return jnp.pad(rows.reshape(B, T, -1), ((0, 0), (0, NEW_PAD - T), (0, 0))).transpose(0, 2, 1)


def _token_minor(cache):
    n, t = cache.shape[:2]
    return jnp.moveaxis(cache, 1, -1).reshape(n, -1, t)


def _head_rows(T):
    kv, t, h = np.meshgrid(np.arange(2), np.arange(T), np.arange(4), indexing="ij")
    return (kv * 4 + h).reshape(-1), t.reshape(-1)


def _scan_logf(x_rows, P):
    B, rows, _ = x_rows.shape
    npg = rows // 8
    y = linear(x_rows.reshape(B * rows, PAGE), P["scan_w"], mode="split3", tn=2 * PAGE, name="scan_within")
    r = np.arange(rows)
    earlier = jnp.asarray(((r[:, None] // 8 > r[None, :] // 8) & (r[:, None] % 8 == r[None, :] % 8)).astype(np.float32), dtype=BF)
    cs = page_prefix(y.reshape(B, rows, 2 * PAGE), earlier)
    return cs.reshape(B, npg, 8, PAGE).transpose(0, 2, 1, 3).reshape(B, 8, npg * PAGE)


def _trunk(x, c, W, P, past):
    B, T, D = x.shape
    N = B * T
    sample = past is not None
    x2 = x.reshape(N, D)
    tm = N if sample else _pick_tile(T, 256)
    mod_rows = N if sample else T
    kw = dict(tm=tm, mod_rows=mod_rows)
    kw_moe = dict(tm=N if sample else _pick_tile(T, MOE_TOKEN_TILE), mod_rows=mod_rows)
    pos0 = past["len"] if sample else 0
    pos = pos0 + jnp.arange(T, dtype=jnp.int32)
    rep = (lambda a: jnp.tile(a, (B, 1))) if sample else (lambda a: a)
    tab_rows = N if sample else T
    out = {}

    sh_a, sc_a, gt_a, sh_m, sc_m, gt_m = _mods(c, W, 0, T, sample)
    c32, s32 = _rope_tabs(pos, MLA_ROPE)
    tabs = [rep(_pad_last(c32, 128)), rep(_pad_last(s32, 128)), rep(jnp.tile(c32, (1, 8))), rep(jnp.tile(s32, (1, 8)))]
    g0 = W["norm_mix"][0][None, :]
    if not sample:
        sbq, sbrows, mla, qn, qr, kn, vv, sbkv = even_front(x2, g0, sc_a, sh_a, P["ev_wext"], tabs, P["ev_gq"], P["ev_gkv"],
                                                             P["ev_wuq"], P["ev_wuk"], P["ev_wuv"], absorbed=False,
                                                             tab_rows=tab_rows, **kw)
        o_sb = gflash(sbq.reshape(B, T, -1), sbkv.reshape(B, T, -1), mode="sb", **ATT_TILES).reshape(N, 512)
        krb = jnp.broadcast_to(mla[:, 256:].astype(BF).reshape(B, T, 1, MLA_ROPE), (B, T, MLA_HEADS, MLA_ROPE))
        qm = jnp.concatenate([qn.reshape(B, T, 8, 64), qr.reshape(B, T, 8, 32)], axis=-1).transpose(0, 2, 1, 3)[:, :, None]
        km = jnp.concatenate([kn.reshape(B, T, 8, 64), krb], axis=-1).transpose(0, 2, 1, 3)
        vm = vv.reshape(B, T, 8, 64).transpose(0, 2, 1, 3)
        o_mla = pflash(qm, km, vm, mode="causal", **MLA_TILES)
        mix = jnp.concatenate([o_sb, _from_heads_major(o_mla, N)], axis=1)
    else:
        pt = past["pt"]
        sbq, sbrows, mla, qlat, qr = even_front(x2, g0, sc_a, sh_a, P["ev_wext"], tabs, P["ev_gq"], P["ev_gkv"], P["ev_wuq"],
                                                P["ev_wabs"], None, absorbed=True, tab_rows=tab_rows, **kw)
        u = jnp.asarray(np.tril(np.ones((PAGE, PAGE), np.float32), -1), dtype=BF)
        n_pool = past["sb"].shape[0]
        o = dec_attn(pt, _block_diag_q(sbq, B, T), _token_minor(past["sb"]), _pad_new(sbrows, B, T), mode="sb",
                     extras=(u,), n_new=T, rows_per_t=4, t_period=4 * T)
        o_sb = _pick_diag(o, B, T).astype(BF)
        qcat = jnp.concatenate([qlat.reshape(B, T, 8, 256), qr.reshape(B, T, 8, 32)], axis=-1).reshape(B, T * 8, 288)
        o_lat = dec_attn(pt, qcat, _token_minor(past["mla"]), _pad_new(mla, B, T), mode="mla", n_new=T,
                         rows_per_t=8, t_period=8 * T)
        o_mla = linear(o_lat.reshape(N, 2048), P["ev_wuv_bd"], out_dtype=BF, name="mla_v_up")
        mix = jnp.concatenate([o_sb, o_mla], axis=1)
    out["sb"] = sbrows.reshape(1, B, T, 2, SB_KV_HEADS, HEAD_DIM)
    out["mla"] = mla.reshape(1, B, T, 288)
    x2 = out_proj_res(mix, P["ev_wout"], x2, gt_a, **kw)
    x2 = moe_layer(x2, W["norm_ffn"][0][None, :], sc_m, sh_m, gt_m, P["wr"][0], P["br"][0],
                   P["moe_wg"], P["moe_wu"], P["moe_wd"], 0, **kw_moe)

    sh_a, sc_a, gt_a, sh_m, sc_m, gt_m = _mods(c, W, 1, T, sample)
    c64, s64 = _rope_tabs(pos, HEAD_DIM)
    tabs = [rep(jnp.tile(c64, (1, 8))), rep(jnp.tile(s64, (1, 8))), rep(jnp.tile(c64, (1, 2))), rep(jnp.tile(s64, (1, 2)))]
    nq, nqf, cmp_rows, slc_rows, win_rows, gates, fq, fox_rows, logf, *kvb = odd_front(
        x2, W["norm_mix"][1][None, :], sc_a, sh_a, P["od_wext"], tabs, P["od_fb"], tab_rows=tab_rows, packed=not sample, **kw)
    logf8 = logf[:, :8]
    G, hg = NSA_GROUPS, NSA_HEADS // NSA_GROUPS
    if not sample:
        nb = T // NSA_BLOCK
        nbp = _round_up(nb, LANES)
        kcv = linear(cmp_rows.astype(BF).reshape(B * nb, NSA_BLOCK * 256), P["cmp_w"], P["cmp_b"], tm=128, tn=256, name="compress")
        kcv = kcv.reshape(B, nb, 2, G, HEAD_DIM).transpose(2, 0, 3, 1, 4)
        kcv = jnp.pad(kcv, ((0, 0), (0, 0), (0, 0), (0, nbp - nb), (0, 0)))
        tq = _pick_tile(T, 128)
        nqt = T // tq
        qf = nqf.reshape(B, nqt, tq, G, hg, HEAD_DIM).transpose(0, 3, 1, 4, 2, 5).reshape(B, G, nqt, hg * tq, HEAD_DIM)
        oc, sel = nsa_cmp_sel(qf, kcv[0], kcv[1], tq=tq, nbc=nb, nbs=nb, pos0=0, bb=2)
        oc = oc.reshape(B, G, nqt, hg, tq, HEAD_DIM).transpose(0, 2, 4, 1, 3, 5).reshape(N, 512)
        sel = sel.reshape(B, G, T, nbp)
        expand = jnp.asarray((np.arange(nbp)[:, None] == np.arange(T)[None, :] // NSA_BLOCK).astype(np.float32), dtype=BF)
        slb, wnb, fxb = (a.reshape(B, T, -1) for a in kvb)
        nq3 = nq.reshape(B, T, -1)
        osel = gflash(nq3, slb, mode="sel", extras=(sel, expand), **ATT_TILES).reshape(N, 512)
        ow = gflash(nq3, wnb, mode="win", **ATT_TILES).reshape(N, 512)
        cs = _scan_logf(logf8.reshape(B, T // PAGE, PAGE, 8).transpose(0, 1, 3, 2).reshape(B, T // PAGE * 8, PAGE), P)
        ch = cs.reshape(B, 2, 4, T)
        cq = jnp.broadcast_to(ch[..., None], (B, 2, 4, T, LANES))
        of = gflash(fq.reshape(B, T, -1), fxb, mode="fox", extras=(cq, ch[:, :, :, None, :]), **ATT_TILES).reshape(N, 512)
        wb = min(NSA_WINDOW, T)
        out["win"] = win_rows.reshape(1, B, T, 2, G, HEAD_DIM)[:, :, T - wb:]
    else:
        n_pages = pt.shape[1]
        L = n_pages * PAGE
        nbc = L // NSA_BLOCK
        nbs = nbc + 1
        nbp = _round_up(nbs, LANES)
        kcv = compress_pages(pt, _token_minor(past["cmp"]), P["cmp_w2"], P["cmp_b2"])
        kcv = kcv.reshape(B, n_pages, 2, G, 2, HEAD_DIM).transpose(2, 0, 3, 1, 4, 5).reshape(2, B, G, nbc, HEAD_DIM)
        kcv = jnp.pad(kcv, ((0, 0), (0, 0), (0, 0), (0, nbp - nbc), (0, 0)))
        qf = nqf.reshape(B, T, G, hg, HEAD_DIM).transpose(0, 2, 3, 1, 4).reshape(B, G, 1, hg * T, HEAD_DIM)
        oc, sel = nsa_cmp_sel(qf, kcv[0], kcv[1], tq=T, nbc=nbc, nbs=nbs, pos0=pos0, bb=4)
        oc = oc.reshape(B, G, hg, T, HEAD_DIM).transpose(0, 3, 1, 2, 4).reshape(N, 512)
        sel = sel.reshape(B, G, T, nbp)
        Gp = _pick_tile(n_pages, DEC_PAGES_PER_STEP)
        nc = n_pages // Gp
        selr = jnp.broadcast_to(sel[:, :, :, None, :], (B, G, T, hg, nbp)).reshape(B, G * T * hg, nbp)
        selx = selr[:, :, :2 * n_pages].reshape(B, G * T * hg, nc, 2 * Gp).transpose(0, 2, 1, 3)
        selnew = jnp.broadcast_to(selr[:, :, 2 * n_pages:2 * n_pages + 1].astype(F32), (B, G * T * hg, NEW_PAD))
        e2 = jnp.asarray((np.arange(2 * Gp)[:, None] == np.arange(Gp * PAGE)[None, :] // NSA_BLOCK).astype(np.float32), dtype=BF)
        rk = dict(n_new=T, rows_per_t=4, t_period=4 * T)
        qbd = _block_diag_q(nq, B, T)
        osel = dec_attn(pt, qbd, _token_minor(past["slc"]), _pad_new(slc_rows, B, T), mode="sel",
                        extras=(selx, e2, selnew), **rk)
        win_buf = past["win"]
        npw = win_buf.shape[1] // PAGE
        pt_win = jnp.zeros((B, npw), jnp.int32)
        ow = dec_attn(pt_win, qbd, _token_minor(win_buf), _pad_new(win_rows, B, T), mode="win", **rk)
        cpast = _scan_logf(gather_page_rows(pt, _token_minor(past["logf"])), P)
        lnew = logf8.reshape(B, T, 8)
        run = cpast[:, :, -1]
        cnew = []
        for tt in range(T):
            run = run + lnew[:, tt, :]
            cnew.append(run)
        cnew = jnp.stack(cnew, axis=1)
        hrow, trow = _head_rows(T)
        ck32 = cpast[:, hrow, :]
        cq32 = jnp.broadcast_to(cnew[:, trow, hrow][:, :, None], (B, len(hrow), LANES))
        cknew = _pad_last(cnew.transpose(0, 2, 1)[:, hrow, :], NEW_PAD)
        of = dec_attn(pt, _block_diag_q(fq, B, T), _token_minor(past["fox"]), _pad_new(fox_rows, B, T), mode="fox",
                      extras=(cq32, ck32, cknew), **rk)
        osel, ow, of = (_pick_diag(a, B, T).astype(BF) for a in (osel, ow, of))
        out["win"] = jnp.concatenate([win_buf, win_rows.reshape(B, T, 2, G, HEAD_DIM)], axis=1)[None, :, T:]
    mix = nsa_combine(oc, osel, ow, gates, P["gate_expand"], of, tm=tm)
    out["cmp"] = cmp_rows.reshape(1, B, T, 2, G, HEAD_DIM)
    out["slc"] = slc_rows.reshape(1, B, T, 2, G, HEAD_DIM)
    out["fox"] = fox_rows.reshape(1, B, T, 2, FOX_KV_HEADS, HEAD_DIM)
    out["logf"] = logf8.reshape(1, B, T, FOX_HEADS)
    x2 = out_proj_res(mix, P["od_wout"], x2, gt_a, **kw)
    x2 = moe_layer(x2, W["norm_ffn"][1][None, :], sc_m, sh_m, gt_m, P["wr"][1], P["br"][1],
                   P["moe_wg"], P["moe_wu"], P["moe_wd"], 1, **kw_moe)
    y = final_norm(x2, W["norm_final"][None, :], tm=tm).reshape(B, T, D)
    return y, out


def kernel(x_prompt, x_sample, c_prompt, c_sample, cache_sb_kv, cache_mla_latent, cache_nsa_cmp_kv, cache_nsa_slc_kv,
           cache_fox_kv, cache_fox_logf, state_nsa_win_kv, page_table, ada_w, ada_b, norm_mix, norm_ffn, norm_final,
           ev_w_in, ev_q_norm, ev_kv_norm, ev_w_uq, ev_w_uk, ev_w_uv, ev_w_out, od_w_in, od_cmp_pe, od_cmp_w, od_fox_fb,
           od_w_out, moe_w_grp, moe_b_grp, moe_w_rt, moe_b_rt, moe_w_gate, moe_w_up, moe_w_down):
    assert ada_w.shape[0] == 2, "two trunk layers (one even, one odd) are implemented"
    W = dict(ada_w=ada_w, ada_b=ada_b, norm_mix=norm_mix, norm_ffn=norm_ffn, norm_final=norm_final, ev_w_in=ev_w_in,
             ev_q_norm=ev_q_norm, ev_kv_norm=ev_kv_norm, ev_w_uq=ev_w_uq, ev_w_uk=ev_w_uk, ev_w_uv=ev_w_uv, ev_w_out=ev_w_out,
             od_w_in=od_w_in, od_cmp_pe=od_cmp_pe, od_cmp_w=od_cmp_w, od_fox_fb=od_fox_fb, od_w_out=od_w_out,
             moe_w_gate=moe_w_gate, moe_w_up=moe_w_up, moe_w_down=moe_w_down)
    P = _prep_weights(W)
    P["wr"] = _pad_last(jnp.concatenate([moe_w_rt, moe_w_grp], axis=-1), LANES)
    P["br"] = _pad_last(jnp.concatenate([moe_b_rt, moe_b_grp], axis=-1), LANES)[:, None, :]
    past = dict(pt=page_table, len=page_table.shape[1] * cache_sb_kv.shape[2], sb=cache_sb_kv[0], mla=cache_mla_latent[0],
                cmp=cache_nsa_cmp_kv[0], slc=cache_nsa_slc_kv[0], fox=cache_fox_kv[0], logf=cache_fox_logf[0],
                win=state_nsa_win_kv[0])
    y_p, sp = _trunk(x_prompt, c_prompt, W, P, None)
    y_s, ss = _trunk(x_sample, c_sample, W, P, past)
    names = ("sb", "mla", "cmp", "slc", "fox", "logf", "win")
    outs = [y_p, y_s]
    for n in names:
        outs += [sp[n], ss[n]]
    return tuple(outs)
```

```python
import functools
import math

import jax
import jax.numpy as jnp
import numpy as np
from jax import lax
from jax.experimental import pallas as pl
from jax.experimental.pallas import tpu as pltpu

F32 = jnp.float32
BF = jnp.bfloat16
HI = lax.Precision.HIGHEST

HEAD_DIM = 64
ROPE_THETA = 10000.0
RMS_EPS = 1e-6
SB_HEADS, SB_KV_HEADS = 8, 2
MLA_HEADS, MLA_NOPE, MLA_ROPE, MLA_V = 8, 64, 32, 64
NSA_HEADS, NSA_GROUPS, NSA_BLOCK, NSA_TOPN, NSA_WINDOW = 8, 2, 64, 16, 512
FOX_HEADS, FOX_KV_HEADS = 8, 2
MOE_GROUPS, MOE_PER_GROUP, MOE_TOPK = 4, 8, 2
MOE_EXPERTS = MOE_GROUPS * MOE_PER_GROUP
MLA_SCALE = 1.0 / math.sqrt(MLA_NOPE + MLA_ROPE)
QK_SCALE = 1.0 / math.sqrt(HEAD_DIM)
LOG2E = math.log2(math.e)
LANES = 128
VMEM_LIMIT_MB = 56
ATT_TILES = dict(tq=256, tk=256)
MLA_TILES = dict(tq=512, tk=512, gb=4)
MOE_TOKEN_TILE = 1024
MOE_EXPERTS_PER_STEP = 4


def _cparams(sem, vmem_mb=VMEM_LIMIT_MB):
    return pltpu.CompilerParams(dimension_semantics=sem, vmem_limit_bytes=vmem_mb * 2 ** 20)


def _round_up(a, m):
    return (a + m - 1) // m * m


def _pick_tile(n, pref):
    t = min(pref, n)
    while n % t:
        t //= 2
    return t


def _rms(x, g):
    return x * lax.rsqrt(jnp.mean(x * x, axis=-1, keepdims=True) + RMS_EPS) * g


def _dot(a, b):
    return jnp.dot(a, b, preferred_element_type=F32)


def _dot_t(a, b, precision=None):
    return lax.dot_general(a, b, (((1,), (1,)), ((), ())), preferred_element_type=F32, precision=precision)


def _split3(x):
    a = x.astype(BF)
    r = x - a.astype(F32)
    b = r.astype(BF)
    c = (r - b.astype(F32)).astype(BF)
    return a, b, c


def _linear_body(*refs, act, mode, has_bias):
    x_ref, w_ref = refs[0], refs[1]
    b_ref = refs[2] if has_bias else None
    o_ref = refs[-1]
    x = x_ref[...]
    if act == "silu":
        x = x.astype(F32)
        x = x * jax.nn.sigmoid(x)
    w = w_ref[...]
    if mode == "hi":
        y = jnp.dot(x.astype(F32), w.astype(F32), precision=HI, preferred_element_type=F32)
    elif mode == "split3":
        a, b, c = _split3(x.astype(F32))
        wb = w.astype(BF)
        y = _dot(a, wb) + _dot(b, wb) + _dot(c, wb)
    else:
        y = _dot(x.astype(BF), w.astype(BF))
    if has_bias:
        y = y + b_ref[...]
    o_ref[...] = y.astype(o_ref.dtype)


def linear(x, w, bias=None, *, widx=None, act=None, mode="bf16", out_dtype=F32, tm=256, tn=512, name="linear"):
    M, K = x.shape
    N = w.shape[-1]
    tm = _pick_tile(M, tm)
    tn = _pick_tile(N, tn)
    if w.ndim == 3:
        w_spec = pl.BlockSpec((None, K, tn), lambda i, j: (widx, 0, j))
    else:
        w_spec = pl.BlockSpec((K, tn), lambda i, j: (0, j))
    in_specs = [pl.BlockSpec((tm, K), lambda i, j: (i, 0)), w_spec]
    args = [x, w]
    if bias is not None:
        if bias.ndim == 3:
            in_specs.append(pl.BlockSpec((None, 1, tn), lambda i, j: (widx, 0, j)))
        else:
            in_specs.append(pl.BlockSpec((1, tn), lambda i, j: (0, j)))
        args.append(bias)
    return pl.pallas_call(
        functools.partial(_linear_body, act=act, mode=mode, has_bias=bias is not None),
        grid=(M // tm, N // tn),
        in_specs=in_specs,
        out_specs=pl.BlockSpec((tm, tn), lambda i, j: (i, j)),
        out_shape=jax.ShapeDtypeStruct((M, N), out_dtype),
        compiler_params=_cparams(("parallel", "parallel")),
        name=name,
    )(*args)


def _mod_spec(tm, D, mod_rows):
    return lambda R: pl.BlockSpec((None, R, D), lambda i: ((i * tm) // mod_rows, 0, 0))


def _even_front_body(x_ref, g_ref, sc_ref, sh_ref, w_ref, c32_ref, s32_ref, c256_ref, s256_ref, gq_ref, gkv_ref,
                     wuq_ref, wa_ref, wb_ref, sbq_ref, sbrows_ref, mla_ref, o1_ref, o2_ref, o3_ref, o4_ref, *, absorbed):
    hn = _rms(x_ref[...], g_ref[...]) * (1.0 + sc_ref[...]) + sh_ref[...]
    proj = _dot(hn.astype(BF), w_ref[...])
    if absorbed:
        sbq_ref[...] = (proj[:, :512] * QK_SCALE).astype(BF)
    else:
        sbq_ref[...] = _pad_heads(proj[:, :512] * QK_SCALE).astype(BF)
        o4_ref[...] = proj[:, 512:768].astype(BF)
    sbrows_ref[...] = proj[:, 512:768]
    cqn = _rms(proj[:, 768:1024], gq_ref[...])
    ckvn = _rms(proj[:, 1024:1280], gkv_ref[...])
    krr = proj[:, 1280:1408] * c32_ref[...] + proj[:, 1408:1536] * s32_ref[...]
    mla_ref[:, :256] = ckvn
    mla_ref[:, 256:288] = krr[:, :32]
    qu = _dot(cqn.astype(BF), wuq_ref[...])
    qr = (qu[:, 512:768] * c256_ref[...] + qu[:, 768:1024] * s256_ref[...]) * MLA_SCALE
    qn = qu[:, :512]
    if absorbed:
        o1_ref[...] = (_dot(qn.astype(BF), wa_ref[...]) * MLA_SCALE).astype(BF)
        o2_ref[...] = qr.astype(BF)
    else:
        ckb = ckvn.astype(BF)
        kn = _dot(ckb, wa_ref[...])
        vv = _dot(ckb, wb_ref[...])
        qs = qn * (MLA_SCALE * LOG2E)
        qr2 = qr * LOG2E
        kr = krr[:, :MLA_ROPE]
        for h in range(MLA_HEADS):
            nope = slice(h * MLA_NOPE, (h + 1) * MLA_NOPE)
            o1_ref[h] = jnp.concatenate([qs[:, nope], qr2[:, h * MLA_ROPE:(h + 1) * MLA_ROPE]], axis=1).astype(BF)
            o2_ref[h] = jnp.concatenate([kn[:, nope], kr], axis=1).astype(BF)
            o3_ref[h] = vv[:, h * MLA_V:(h + 1) * MLA_V].astype(BF)


def even_front(x2, g, sc, sh, wext, tabs, gq, gkv, wuq, wa, wb, *, tm, mod_rows, tab_rows, absorbed):
    N, D = x2.shape
    R = sc.shape[1]
    c32, s32, c256, s256 = tabs
    nt = tab_rows // tm
    row = lambda w: pl.BlockSpec((tm, w), lambda i: (i, 0))
    tab = lambda w: pl.BlockSpec((tm, w), lambda i: (i % nt, 0))
    full = lambda a: pl.BlockSpec(a.shape, lambda i: (0,) * a.ndim)
    mod = _mod_spec(tm, D, mod_rows)(R)
    in_specs = [row(D), full(g), mod, mod, full(wext), tab(128), tab(128), tab(256), tab(256), full(gq), full(gkv),
                full(wuq), full(wa)]
    args = [x2, g, sc, sh, wext, c32, s32, c256, s256, gq, gkv, wuq, wa]
    if absorbed:
        body = lambda *r: _even_front_body(*r[:13], None, *r[13:], None, None, absorbed=True)
        outs = [(512, BF), (256, F32), (288, F32), (2048, BF), (256, BF)]
        out_specs = [row(w) for w, _ in outs]
        out_shape = [jax.ShapeDtypeStruct((N, w), dt) for w, dt in outs]
    else:
        in_specs.append(full(wb))
        args.append(wb)
        body = functools.partial(_even_front_body, absorbed=False)
        tps = mod_rows // tm
        B = N // mod_rows
        heads = lambda w: pl.BlockSpec((None, MLA_HEADS, tm, w), lambda i: (i // tps, 0, i % tps, 0))
        hshape = lambda w: jax.ShapeDtypeStruct((B, MLA_HEADS, mod_rows, w), BF)
        dk = MLA_NOPE + MLA_ROPE
        out_specs = [row(1024), row(256), row(288), heads(dk), heads(dk), heads(MLA_V), row(256)]
        out_shape = [jax.ShapeDtypeStruct((N, 1024), BF), jax.ShapeDtypeStruct((N, 256), F32), jax.ShapeDtypeStruct((N, 288), F32),
                     hshape(dk), hshape(dk), hshape(MLA_V), jax.ShapeDtypeStruct((N, 256), BF)]
    return pl.pallas_call(
        body,
        grid=(N // tm,),
        in_specs=in_specs,
        out_specs=out_specs,
        out_shape=out_shape,
        compiler_params=_cparams(("parallel",)),
        name="even_front",
    )(*args)


def _log_sigmoid(z):
    return jnp.minimum(z, 0.0) - jnp.log(1.0 + jnp.exp(-jnp.abs(z)))


def _odd_front_body(x_ref, g_ref, sc_ref, sh_ref, w_ref, c512_ref, s512_ref, c128_ref, s128_ref, fb_ref,
                    nq_ref, nqf_ref, cmp_ref, slc_ref, win_ref, gates_ref, fq_ref, fox_ref, logf_ref, *kvb_refs, packed):
    hn = _rms(x_ref[...], g_ref[...]) * (1.0 + sc_ref[...]) + sh_ref[...]
    p = _dot(hn.astype(BF), w_ref[...])
    nq = (p[:, 0:512] * c512_ref[...] + p[:, 512:1024] * s512_ref[...]) * QK_SCALE
    nq_ref[...] = (_pad_heads(nq * LOG2E) if packed else nq).astype(BF)
    nqf_ref[...] = nq
    c128, s128 = c128_ref[...], s128_ref[...]
    cmp_ref[:, :128] = p[:, 1024:1152] * c128 + p[:, 1152:1280] * s128
    cmp_ref[:, 128:] = p[:, 1280:1408]
    slc_ref[:, :128] = p[:, 1408:1536] * c128 + p[:, 1536:1664] * s128
    slc_ref[:, 128:] = p[:, 1664:1792]
    win_ref[:, :128] = p[:, 1792:1920] * c128 + p[:, 1920:2048] * s128
    win_ref[:, 128:] = p[:, 2048:2176]
    gates_ref[...] = jax.nn.sigmoid(p[:, 2176:2304])
    fq = p[:, 2304:2816] * QK_SCALE
    fq_ref[...] = (_pad_heads(fq * LOG2E) if packed else fq).astype(BF)
    fox_ref[...] = p[:, 2816:3072]
    logf_ref[...] = _log_sigmoid(p[:, 3072:3200] + fb_ref[...])
    if packed:
        for src, dst in zip((slc_ref, win_ref, fox_ref), kvb_refs):
            dst[...] = src[...].astype(BF)


def odd_front(x2, g, sc, sh, wext, tabs, fb, *, tm, mod_rows, tab_rows, packed):
    N, D = x2.shape
    R = sc.shape[1]
    nt = tab_rows // tm
    row = lambda w: pl.BlockSpec((tm, w), lambda i: (i, 0))
    tab = lambda w: pl.BlockSpec((tm, w), lambda i: (i % nt, 0))
    full = lambda a: pl.BlockSpec(a.shape, lambda i: (0,) * a.ndim)
    mod = _mod_spec(tm, D, mod_rows)(R)
    qw = 1024 if packed else 512
    outs = [(qw, BF), (512, F32), (256, F32), (256, F32), (256, F32), (128, F32), (qw, BF), (256, F32), (128, F32)]
    if packed:
        outs += [(256, BF)] * 3
    return pl.pallas_call(
        functools.partial(_odd_front_body, packed=packed),
        grid=(N // tm,),
        in_specs=[row(D), full(g), mod, mod, full(wext), tab(512), tab(512), tab(128), tab(128), full(fb)],
        out_specs=[row(w) for w, _ in outs],
        out_shape=[jax.ShapeDtypeStruct((N, w), dt) for w, dt in outs],
        compiler_params=_cparams(("parallel",)),
        name="odd_front",
    )(x2, g, sc, sh, wext, *tabs, fb)


def _outproj_body(mix_ref, w_ref, x_ref, gt_ref, o_ref):
    o_ref[...] = x_ref[...] + gt_ref[...] * _dot(mix_ref[...].astype(BF), w_ref[...])


def out_proj_res(mix, w, x2, gt, *, tm, mod_rows):
    N, D = x2.shape
    K = mix.shape[1]
    return pl.pallas_call(
        _outproj_body,
        grid=(N // tm,),
        in_specs=[pl.BlockSpec((tm, K), lambda i: (i, 0)), pl.BlockSpec(w.shape, lambda i: (0, 0)),
                  pl.BlockSpec((tm, D), lambda i: (i, 0)), _mod_spec(tm, D, mod_rows)(gt.shape[1])],
        out_specs=pl.BlockSpec((tm, D), lambda i: (i, 0)),
        out_shape=jax.ShapeDtypeStruct((N, D), F32),
        compiler_params=_cparams(("parallel",)),
        name="out_proj_res",
    )(mix, w, x2, gt)


def _nsa_combine_body(oc_ref, os_ref, ow_ref, g_ref, e_ref, of_ref, o_ref):
    a, b, c = _split3(g_ref[...])
    e = e_ref[...]
    gx = _dot(a, e) + _dot(b, e) + _dot(c, e)
    o = (gx[:, :512] * oc_ref[...].astype(F32) + gx[:, 512:1024] * os_ref[...].astype(F32)
         + gx[:, 1024:] * ow_ref[...].astype(F32))
    o_ref[:, :512] = o.astype(BF)
    o_ref[:, 512:] = of_ref[...]


def nsa_combine(oc, osel, ow, gates, expand, of, *, tm):
    N = oc.shape[0]
    row = lambda w: pl.BlockSpec((tm, w), lambda i: (i, 0))
    return pl.pallas_call(
        _nsa_combine_body,
        grid=(N // tm,),
        in_specs=[row(512), row(512), row(512), row(128), pl.BlockSpec(expand.shape, lambda i: (0, 0)), row(512)],
        out_specs=row(1024),
        out_shape=jax.ShapeDtypeStruct((N, 1024), BF),
        compiler_params=_cparams(("parallel",)),
        name="nsa_combine",
    )(oc, osel, ow, gates, expand, of)


def _final_norm_body(x_ref, g_ref, o_ref):
    o_ref[...] = _rms(x_ref[...], g_ref[...])


def final_norm(x2, g, *, tm):
    N, D = x2.shape
    return pl.pallas_call(
        _final_norm_body,
        grid=(N // tm,),
        in_specs=[pl.BlockSpec((tm, D), lambda i: (i, 0)), pl.BlockSpec((1, D), lambda i: (0, 0))],
        out_specs=pl.BlockSpec((tm, D), lambda i: (i, 0)),
        out_shape=jax.ShapeDtypeStruct((N, D), F32),
        compiler_params=_cparams(("parallel",)),
        name="final_norm",
    )(x2, g)


def _moe_body(x_ref, g_ref, sc_ref, sh_ref, gt_ref, wr_ref, br_ref, wg_ref, wu_ref, wd_ref, o_ref, hn_s, gates_s, acc_s):
    e = pl.program_id(1)
    lane = lax.broadcasted_iota(jnp.int32, gates_s.shape, 1)
    lanef = lane.astype(F32)

    @pl.when(e == 0)
    def _():
        hn = _rms(x_ref[...], g_ref[...]) * (1.0 + sc_ref[...]) + sh_ref[...]
        hn_s[...] = hn.astype(BF)
        h_hi = hn.astype(BF)
        h_lo = (hn - h_hi.astype(F32)).astype(BF)
        w = wr_ref[...]
        w_hi = w.astype(BF)
        w_lo = (w - w_hi.astype(F32)).astype(BF)
        lr = _dot(h_hi, w_hi) + _dot(h_lo, w_hi) + _dot(h_hi, w_lo) + br_ref[...]
        lg = jnp.where((lane >= MOE_EXPERTS) & (lane < MOE_EXPERTS + MOE_GROUPS), lr, -jnp.inf)
        mg = jnp.max(lg, axis=1, keepdims=True)
        g_sel = jnp.min(jnp.where(lg == mg, lanef, 1e9), axis=1, keepdims=True) - MOE_EXPERTS
        g_w = 1.0 / jnp.sum(jnp.exp(lg - mg), axis=1, keepdims=True)
        ingrp = (lane >> 3).astype(F32) == g_sel
        ingrp = ingrp & (lane < MOE_EXPERTS)
        lrm = jnp.where(ingrp, lr, -jnp.inf)
        m1 = jnp.max(lrm, axis=1, keepdims=True)
        ex = jnp.exp(lrm - m1)
        i1 = jnp.min(jnp.where(ingrp & (ex == 1.0), lanef, 1e9), axis=1, keepdims=True)
        c2 = jnp.where(ingrp & (lanef != i1), ex, -1.0)
        v2 = jnp.max(c2, axis=1, keepdims=True)
        i2 = jnp.min(jnp.where(c2 == v2, lanef, 1e9), axis=1, keepdims=True)
        den = 1.0 + v2
        gates_s[...] = jnp.where(lanef == i1, g_w / den, 0.0) + jnp.where(lanef == i2, g_w * v2 / den, 0.0)
        acc_s[...] = jnp.zeros_like(acc_s)

    hb = hn_s[...]
    gates = gates_s[...]
    hids = []
    for k in range(MOE_EXPERTS_PER_STEP):
        ge = jnp.sum(jnp.where(lane == e * MOE_EXPERTS_PER_STEP + k, gates, 0.0), axis=1, keepdims=True)
        a = _dot(hb, wg_ref[k])
        u = _dot(hb, wu_ref[k])
        hids.append((a * jax.nn.sigmoid(a) * u * ge).astype(BF))
    wd = wd_ref[...]
    acc_s[...] += _dot(jnp.concatenate(hids, axis=1), wd.reshape(wd.shape[0] * wd.shape[1], wd.shape[2]))

    @pl.when(e == pl.num_programs(1) - 1)
    def _():
        o_ref[...] = x_ref[...] + gt_ref[...] * acc_s[...]


def moe_layer(x2, g, sc, sh, gt, wr, br, wg, wu, wd, l, *, tm, mod_rows):
    N, D = x2.shape
    Hh = wg.shape[-1]
    row = pl.BlockSpec((tm, D), lambda i, e: (i, 0))
    mod = lambda a: pl.BlockSpec((None, a.shape[1], D), lambda i, e: ((i * tm) // mod_rows, 0, 0))
    full = lambda a: pl.BlockSpec(a.shape, lambda i, e: (0,) * a.ndim)
    return pl.pallas_call(
        _moe_body,
        grid=(N // tm, MOE_EXPERTS // MOE_EXPERTS_PER_STEP),
        in_specs=[row, full(g), mod(sc), mod(sh), mod(gt), full(wr), full(br),
                  pl.BlockSpec((None, MOE_EXPERTS_PER_STEP, D, Hh), lambda i, e: (l, e, 0, 0)),
                  pl.BlockSpec((None, MOE_EXPERTS_PER_STEP, D, Hh), lambda i, e: (l, e, 0, 0)),
                  pl.BlockSpec((None, MOE_EXPERTS_PER_STEP, Hh, D), lambda i, e: (l, e, 0, 0))],
        out_specs=row,
        out_shape=jax.ShapeDtypeStruct((N, D), F32),
        scratch_shapes=[pltpu.VMEM((tm, D), BF), pltpu.VMEM((tm, LANES), F32), pltpu.VMEM((tm, D), F32)],
        compiler_params=_cparams(("parallel", "arbitrary")),
        name="moe",
    )(x2, g, sc, sh, gt, wr, br, wg, wu, wd)


def _softplus(z):
    return jnp.maximum(z, 0.0) + jnp.log(1.0 + jnp.exp(-jnp.abs(z)))


def _pflash_body(*refs, mode, gb, hg, tq, tk, nk, dv, window):
    it = iter(refs)
    q_ref, k_ref, v_ref = next(it), next(it), next(it)
    cq_ref = ck_ref = sel_ref = e_ref = u_ref = None
    if mode == "fox":
        cq_ref, ck_ref = next(it), next(it)
    if mode == "sel":
        sel_ref, e_ref = next(it), next(it)
    if mode == "sb":
        u_ref = next(it)
    o_ref, m_s, acc_s = next(it), next(it), next(it)
    i, j = pl.program_id(2), pl.program_id(3)
    R = hg * tq
    last = (i * tq + tq - 1) // tk

    @pl.when(j == 0)
    def _():
        m_s[...] = jnp.full(m_s.shape, 0.0 if mode == "sb" else -jnp.inf, F32)
        acc_s[...] = jnp.zeros_like(acc_s)

    if mode == "win":
        kt = last - (nk - 1) + j
        active = kt >= 0
        edge = active
    elif mode == "sb":
        kt = last - j
        active = kt >= 0
        edge = j == 0
    else:
        kt = j
        active = j <= last
        edge = j == last

    def tile(gi, positional):
        q = q_ref[gi].reshape(R, q_ref.shape[-1])
        s3 = _dot_t(q, k_ref[gi]).reshape(hg, tq, tk)
        mask = None
        if positional:
            qpos = i * tq + lax.broadcasted_iota(jnp.int32, (tq, tk), 0)
            kpos = kt * tk + lax.broadcasted_iota(jnp.int32, (tq, tk), 1)
            if mode == "sb":
                mask = kpos < qpos
            elif mode == "win":
                d = qpos - kpos
                mask = (d >= 0) & (d < window)
            else:
                mask = kpos <= qpos
        if mode == "sel":
            picked = _dot(sel_ref[gi], e_ref[...]) > 0.5
            mask = picked if mask is None else mask & picked
        if mode == "fox":
            s3 = s3 - ck_ref[gi]
        chunk = lambda a: [a[:, c * LANES:(c + 1) * LANES] for c in range(tk // LANES)]
        if mode == "sb":
            lk3 = -_softplus(s3)
            if mask is not None:
                lk3 = jnp.where(mask[None], lk3, 0.0)
            lk = lk3.reshape(R, tk)
            hi = lk.astype(BF)
            lo = (lk - hi.astype(F32)).astype(BF)
            u = u_ref[...]
            sufx = _dot(hi, u) + _dot(lo, u)
            carry = m_s[gi]
            e = jnp.concatenate([a + b + carry for a, b in zip(chunk((s3 + lk3).reshape(R, tk)), chunk(sufx))], axis=1)
            w3 = jnp.exp(e.reshape(hg, tq, tk))
            if mask is not None:
                w3 = jnp.where(mask[None], w3, 0.0)
            acc_s[gi] += _dot(w3.reshape(R, tk).astype(BF), v_ref[gi])
            m_s[gi] = carry + sufx[:, tk:]
        else:
            if mask is not None:
                s3 = jnp.where(mask[None], s3, -jnp.inf)
            chunks = chunk(s3.reshape(R, tk))
            if mode == "fox":
                cq = cq_ref[gi].reshape(R, LANES)
                chunks = [a + cq for a in chunks]
            m_prev = m_s[gi]
            m_new = jnp.maximum(m_prev, jnp.max(functools.reduce(jnp.maximum, chunks), axis=1, keepdims=True))
            m_use = jnp.where(m_new > -jnp.inf, m_new, 0.0)
            p = jnp.concatenate([jnp.exp2(a - m_use) for a in chunks], axis=1).astype(BF)
            acc_s[gi] = jnp.exp2(m_prev - m_use) * acc_s[gi] + _dot(p, v_ref[gi])
            m_s[gi] = m_new

    @pl.when(active & edge)
    def _():
        for gi in range(gb):
            tile(gi, True)

    if mode != "win":
        @pl.when(active & jnp.logical_not(edge))
        def _():
            for gi in range(gb):
                tile(gi, False)

    @pl.when(j == nk - 1)
    def _():
        acc = acc_s[...]
        out = acc[:, :, :dv]
        if mode != "sb":
            l = acc[:, :, dv:dv + 1]
            out = out / jnp.where(l > 0.0, l, 1.0)
        o_ref[...] = out.reshape(o_ref.shape).astype(o_ref.dtype)


def _ones_column(v):
    pad = LANES - v.shape[-1] - 1
    return jnp.concatenate([v, jnp.ones(v.shape[:-1] + (1,), v.dtype), jnp.zeros(v.shape[:-1] + (pad,), v.dtype)], axis=-1)


def pflash(q, k, v, *, mode, tq, tk, gb=2, extras=(), window=NSA_WINDOW):
    B, G, hg, T, dk = q.shape
    dv = v.shape[-1]
    tq, tk = _pick_tile(T, tq), _pick_tile(T, tk)
    assert tk % LANES == 0 and tk % tq == 0
    gb = _pick_tile(G, gb)
    nq = T // tq
    last = lambda i: (i * tq + tq - 1) // tk
    if mode == "win":
        nk = max(last(i) - max(i * tq - window + 1, 0) // tk + 1 for i in range(nq))
        kidx = lambda i, j: jnp.maximum(last(i) - (nk - 1) + j, 0)
    elif mode == "sb":
        nk = T // tk
        kidx = lambda i, j: jnp.maximum(last(i) - j, 0)
    else:
        nk = T // tk
        kidx = lambda i, j: jnp.minimum(j, last(i))
    in_specs = [pl.BlockSpec((None, gb, hg, tq, dk), lambda b, g, i, j: (b, g, 0, i, 0)),
                pl.BlockSpec((None, gb, tk, dk), lambda b, g, i, j: (b, g, kidx(i, j), 0)),
                pl.BlockSpec((None, gb, tk, LANES), lambda b, g, i, j: (b, g, kidx(i, j), 0))]
    if mode == "fox":
        in_specs += [pl.BlockSpec((None, gb, hg, tq, LANES), lambda b, g, i, j: (b, g, 0, i, 0)),
                     pl.BlockSpec((None, gb, hg, 1, tk), lambda b, g, i, j: (b, g, 0, 0, kidx(i, j)))]
    if mode == "sel":
        nbp = extras[0].shape[-1]
        in_specs += [pl.BlockSpec((None, gb, tq, nbp), lambda b, g, i, j: (b, g, i, 0)),
                     pl.BlockSpec((nbp, tk), lambda b, g, i, j: (0, kidx(i, j)))]
    if mode == "sb":
        extras = (jnp.asarray(np.concatenate([np.tril(np.ones((tk, tk), np.float32), -1), np.ones((tk, LANES), np.float32)],
                                             axis=1), dtype=BF),)
        in_specs += [pl.BlockSpec((tk, tk + LANES), lambda b, g, i, j: (0, 0))]
    R = hg * tq
    return pl.pallas_call(
        functools.partial(_pflash_body, mode=mode, gb=gb, hg=hg, tq=tq, tk=tk, nk=nk, dv=dv, window=window),
        grid=(B, G // gb, nq, nk),
        in_specs=in_specs,
        out_specs=pl.BlockSpec((None, gb, hg, tq, dv), lambda b, g, i, j: (b, g, 0, i, 0)),
        out_shape=jax.ShapeDtypeStruct((B, G, hg, T, dv), BF),
        scratch_shapes=[pltpu.VMEM((gb, R, LANES), F32), pltpu.VMEM((gb, R, LANES), F32)],
        compiler_params=_cparams(("parallel", "parallel", "parallel", "arbitrary")),
        name="pflash_" + mode,
    )(q, k, _ones_column(v), *extras)


def _pad_heads(q):
    low = lax.broadcasted_iota(jnp.int32, (q.shape[0], LANES), 1) < HEAD_DIM
    tiles = []
    for h in range(8):
        t = q[:, (h // 2) * LANES:(h // 2 + 1) * LANES]
        if h % 2 != h // 4:
            t = pltpu.roll(t, HEAD_DIM, 1)
        tiles.append(jnp.where(low if h < 4 else jnp.logical_not(low), t, 0.0))
    return jnp.concatenate(tiles, axis=1)


def _gflash_body(*refs, mode, hg, tq, tk, nk, window):
    it = iter(refs)
    q_ref, k_ref, v_ref = next(it), next(it), next(it)
    cq_ref = ck_ref = sel_ref = e_ref = u_ref = None
    if mode == "fox":
        cq_ref, ck_ref = next(it), next(it)
    if mode == "sel":
        sel_ref, e_ref = next(it), next(it)
    if mode == "sb":
        u_ref = next(it)
    o_ref, m_s, l_s, acc_s = next(it), next(it), next(it), next(it)
    i, j = pl.program_id(1), pl.program_id(2)
    R = hg * tq
    last = (i * tq + tq - 1) // tk

    @pl.when(j == 0)
    def _():
        m_s[...] = jnp.full(m_s.shape, 0.0 if mode == "sb" else -jnp.inf, F32)
        l_s[...] = jnp.zeros_like(l_s)
        acc_s[...] = jnp.zeros_like(acc_s)

    if mode == "win":
        kt = last - (nk - 1) + j
        active = kt >= 0
        edge = active
    elif mode == "sb":
        kt = last - j
        active = kt >= 0
        edge = j == 0
    else:
        kt = j
        active = j <= last
        edge = j == last

    def tile(g, positional):
        q = jnp.concatenate([q_ref[:, (g * hg + h) * LANES:(g * hg + h + 1) * LANES] for h in range(hg)], axis=0)
        s3 = _dot_t(q, k_ref[...]).reshape(hg, tq, tk)
        mask = None
        if positional:
            qpos = i * tq + lax.broadcasted_iota(jnp.int32, (tq, tk), 0)
            kpos = kt * tk + lax.broadcasted_iota(jnp.int32, (tq, tk), 1)
            if mode == "sb":
                mask = kpos < qpos
            elif mode == "win":
                d = qpos - kpos
                mask = (d >= 0) & (d < window)
            else:
                mask = kpos <= qpos
        if mode == "sel":
            picked = _dot(sel_ref[g], e_ref[...]) > 0.5
            mask = picked if mask is None else mask & picked
        if mode == "fox":
            s3 = s3 - ck_ref[g]
        chunk = lambda a: [a[:, c * LANES:(c + 1) * LANES] for c in range(tk // LANES)]
        if mode == "sb":
            lk3 = -_softplus(s3)
            if mask is not None:
                lk3 = jnp.where(mask[None], lk3, 0.0)
            sufx = _dot(lk3.reshape(R, tk).astype(BF), u_ref[...])
            carry = m_s[g]
            e = jnp.concatenate([a + b + carry for a, b in zip(chunk((s3 + lk3).reshape(R, tk)), chunk(sufx))], axis=1)
            w3 = jnp.exp(e.reshape(hg, tq, tk))
            if mask is not None:
                w3 = jnp.where(mask[None], w3, 0.0)
            acc_s[g] += _dot(w3.reshape(R, tk).astype(BF), v_ref[...])
            m_s[g] = carry + sufx[:, tk:]
        else:
            if mask is not None:
                s3 = jnp.where(mask[None], s3, -jnp.inf)
            chunks = chunk(s3.reshape(R, tk))
            if mode == "fox":
                cq = cq_ref[g].reshape(R, LANES)
                chunks = [a + cq for a in chunks]
            m_prev = m_s[g]
            m_new = jnp.maximum(m_prev, jnp.max(functools.reduce(jnp.maximum, chunks), axis=1, keepdims=True))
            m_use = jnp.where(m_new > -jnp.inf, m_new, 0.0)
            p = jnp.concatenate([jnp.exp2(a - m_use) for a in chunks], axis=1).astype(BF)
            alpha = jnp.exp2(m_prev - m_use)
            l_s[g] = alpha * l_s[g] + _dot(p, jnp.ones((tk, LANES), BF))
            acc_s[g] = alpha * acc_s[g] + _dot(p, v_ref[...])
            m_s[g] = m_new

    @pl.when(active & edge)
    def _():
        for g in range(2):
            tile(g, True)

    if mode != "win":
        @pl.when(active & jnp.logical_not(edge))
        def _():
            for g in range(2):
                tile(g, False)

    @pl.when(j == nk - 1)
    def _():
        for g in range(2):
            acc = acc_s[g]
            if mode != "sb":
                l = l_s[g]
                acc = acc / jnp.where(l > 0.0, l, 1.0)
            heads = [acc[h * tq:(h + 1) * tq, g * HEAD_DIM:(g + 1) * HEAD_DIM] for h in range(hg)]
            o_ref[:, g * hg * HEAD_DIM:(g + 1) * hg * HEAD_DIM] = jnp.concatenate(heads, axis=1).astype(o_ref.dtype)


def gflash(qp, kv, *, mode, tq, tk, extras=(), window=NSA_WINDOW):
    B, T, _ = qp.shape
    hg = 4
    tq, tk = _pick_tile(T, tq), _pick_tile(T, tk)
    assert tk % LANES == 0 and tk % tq == 0
    nq = T // tq
    last = lambda i: (i * tq + tq - 1) // tk
    if mode == "win":
        nk = max(last(i) - max(i * tq - window + 1, 0) // tk + 1 for i in range(nq))
        kidx = lambda i, j: jnp.maximum(last(i) - (nk - 1) + j, 0)
    elif mode == "sb":
        nk = T // tk
        kidx = lambda i, j: jnp.maximum(last(i) - j, 0)
    else:
        nk = T // tk
        kidx = lambda i, j: jnp.minimum(j, last(i))
    in_specs = [pl.BlockSpec((None, tq, 8 * LANES), lambda b, i, j: (b, i, 0)),
                pl.BlockSpec((None, tk, LANES), lambda b, i, j: (b, kidx(i, j), 0)),
                pl.BlockSpec((None, tk, LANES), lambda b, i, j: (b, kidx(i, j), 1))]
    if mode == "fox":
        in_specs += [pl.BlockSpec((None, 2, hg, tq, LANES), lambda b, i, j: (b, 0, 0, i, 0)),
                     pl.BlockSpec((None, 2, hg, 1, tk), lambda b, i, j: (b, 0, 0, 0, kidx(i, j)))]
    if mode == "sel":
        nbp = extras[0].shape[-1]
        in_specs += [pl.BlockSpec((None, 2, tq, nbp), lambda b, i, j: (b, 0, i, 0)),
                     pl.BlockSpec((nbp, tk), lambda b, i, j: (0, kidx(i, j)))]
    if mode == "sb":
        extras = (jnp.asarray(np.concatenate([np.tril(np.ones((tk, tk), np.float32), -1), np.ones((tk, LANES), np.float32)],
                                             axis=1), dtype=BF),)
        in_specs += [pl.BlockSpec((tk, tk + LANES), lambda b, i, j: (0, 0))]
    R = hg * tq
    return pl.pallas_call(
        functools.partial(_gflash_body, mode=mode, hg=hg, tq=tq, tk=tk, nk=nk, window=window),
        grid=(B, nq, nk),
        in_specs=in_specs,
        out_specs=pl.BlockSpec((None, tq, 2 * hg * HEAD_DIM), lambda b, i, j: (b, i, 0)),
        out_shape=jax.ShapeDtypeStruct((B, T, 2 * hg * HEAD_DIM), BF),
        scratch_shapes=[pltpu.VMEM((2, R, LANES), F32), pltpu.VMEM((2, R, LANES), F32), pltpu.VMEM((2, R, LANES), F32)],
        compiler_params=_cparams(("parallel", "parallel", "arbitrary")),
        name="gflash_" + mode,
    )(qp, kv, kv, *extras)


def _cmpsel_body(q_ref, kc_ref, vc_ref, oc_ref, sel_ref, *, hg, tq, nbc, nbs, n_sel, pos0):
    for bi in range(q_ref.shape[0]):
        for g in range(q_ref.shape[1]):
            _cmpsel_chain(q_ref.at[bi, g], kc_ref.at[bi, g], vc_ref.at[bi, g], oc_ref.at[bi, g], sel_ref.at[bi, g],
                          hg=hg, tq=tq, nbc=nbc, nbs=nbs, n_sel=n_sel, pos0=pos0)


def _cmpsel_chain(q_ref, kc_ref, vc_ref, oc_ref, sel_ref, *, hg, tq, nbc, nbs, n_sel, pos0):
    qi = pl.program_id(1)
    R = hg * tq
    nbp = kc_ref.shape[0]
    s = _dot_t(q_ref[...], kc_ref[...], precision=HI)
    n_r = lax.broadcasted_iota(jnp.int32, (R, nbp), 1)
    r_r = lax.broadcasted_iota(jnp.int32, (R, nbp), 0)
    qpos_r = pos0 + qi * tq + (r_r % tq)
    cmask = (n_r * NSA_BLOCK + (NSA_BLOCK - 1) <= qpos_r) & (n_r < nbc)
    s = jnp.where(cmask, s, -jnp.inf)
    m = jnp.max(s, axis=1, keepdims=True)
    m = jnp.where(m > -jnp.inf, m, 0.0)
    ex = jnp.exp(s - m)
    den = jnp.sum(ex, axis=1, keepdims=True)
    p = ex / jnp.where(den > 0.0, den, 1.0)
    oc_ref[...] = _dot(p.astype(BF), vc_ref[...].astype(BF)).astype(oc_ref.dtype)
    tt = lax.broadcasted_iota(jnp.int32, (tq, R), 0)
    rr = lax.broadcasted_iota(jnp.int32, (tq, R), 1)
    fold = (rr % tq == tt).astype(BF)
    a, b, c = _split3(p)
    imp = _dot(fold, a) + _dot(fold, b) + _dot(fold, c)
    n = lax.broadcasted_iota(jnp.int32, (tq, nbp), 1)
    qpos = pos0 + qi * tq + lax.broadcasted_iota(jnp.int32, (tq, nbp), 0)
    cur = qpos // NSA_BLOCK
    forced = (n == 0) | (n == cur) | (n == cur - 1)
    score = jnp.where(forced, float(hg + 1), jnp.where(n <= cur, imp, -1.0))
    score = jnp.where(n < nbs, score, -2.0)
    nf = n.astype(F32)
    sel = jnp.zeros((tq, nbp), F32)
    for _ in range(n_sel):
        mx = jnp.max(score, axis=1, keepdims=True)
        idx = jnp.min(jnp.where(score == mx, nf, 1e9), axis=1, keepdims=True)
        hit = nf == idx
        sel = jnp.where(hit & (mx >= 0.0), 1.0, sel)
        score = jnp.where(hit, -3.0, score)
    sel_ref[...] = sel.astype(sel_ref.dtype)


def nsa_cmp_sel(qf, kc, vc, *, tq, nbc, nbs, pos0, bb):
    B, G, nqt, R, d = qf.shape
    hg = R // tq
    nbp = kc.shape[2]
    n_sel = min(NSA_TOPN, nbs)
    bb = _pick_tile(B, bb)
    return pl.pallas_call(
        functools.partial(_cmpsel_body, hg=hg, tq=tq, nbc=nbc, nbs=nbs, n_sel=n_sel, pos0=pos0),
        grid=(B // bb, nqt),
        in_specs=[pl.BlockSpec((bb, G, None, R, d), lambda b, i: (b, 0, i, 0, 0)),
                  pl.BlockSpec((bb, G, nbp, d), lambda b, i: (b, 0, 0, 0)),
                  pl.BlockSpec((bb, G, nbp, d), lambda b, i: (b, 0, 0, 0))],
        out_specs=[pl.BlockSpec((bb, G, None, R, d), lambda b, i: (b, 0, i, 0, 0)),
                   pl.BlockSpec((bb, G, None, tq, nbp), lambda b, i: (b, 0, i, 0, 0))],
        out_shape=[jax.ShapeDtypeStruct((B, G, nqt, R, d), BF), jax.ShapeDtypeStruct((B, G, nqt, tq, nbp), BF)],
        compiler_params=_cparams(("parallel", "parallel")),
        name="nsa_cmp_sel",
    )(qf, kc, vc)


PAGE = 128
DEC_PAGES_PER_STEP = 16
DEC_SEQS_PER_STEP = 4
NEW_PAD = PAGE


def _dec_softmax_step(s, vs, m_s, l_s, acc_s):
    m_prev = m_s[:, :1]
    m_new = jnp.maximum(m_prev, jnp.max(s, axis=1, keepdims=True))
    m_use = jnp.where(m_new > -jnp.inf, m_new, 0.0)
    p = jnp.exp(s - m_use)
    alpha = jnp.exp(m_prev - m_use)
    l_s[...] = jnp.broadcast_to(alpha * l_s[:, :1] + jnp.sum(p, axis=1, keepdims=True), l_s.shape)
    w = vs[0].shape[1]
    pv = _dot_t(p[:, :w].astype(BF), vs[0])
    for n in range(1, len(vs)):
        pv = pv + _dot_t(p[:, n * w:(n + 1) * w].astype(BF), vs[n])
    acc_s[...] = alpha * acc_s[...] + pv
    m_s[...] = jnp.broadcast_to(m_new, m_s.shape)


def _dec_body(pt_ref, *refs, mode, G, bb, nc, n_new, rows_per_t, t_period, wlen, window):
    del pt_ref
    it = iter(refs)
    q_ref = next(it)
    cq_ref = ck_ref = cknew_ref = selx_ref = e2_ref = selnew_ref = u_ref = None
    if mode == "fox":
        cq_ref, ck_ref, cknew_ref = next(it), next(it), next(it)
    if mode == "sel":
        selx_ref, e2_ref, selnew_ref = next(it), next(it), next(it)
    if mode == "sb":
        u_ref = next(it)
    pages = [[next(it) for _ in range(G)] for _ in range(bb)]
    new_ref = next(it)
    o_ref, m_s, l_s, acc_s = next(it), next(it), next(it), next(it)
    c = pl.program_id(1)
    R = q_ref.shape[1]

    def row_t(shape):
        r = lax.broadcasted_iota(jnp.int32, shape, 0)
        return (r % t_period) // rows_per_t

    def split_kv(x):
        if mode == "mla":
            kb = x.astype(BF)
            return kb, kb[:256]
        return x[:128].astype(BF), x[128:].astype(BF)

    def new_rows(bi):
        q = q_ref[bi]
        kb, vb = split_kv(new_ref[bi])
        s = _dot(q, kb)
        t = row_t(s.shape)
        key = lax.broadcasted_iota(jnp.int32, s.shape, 1)
        if mode == "sb":
            mask = (key < t) & (key < n_new)
            lk = jnp.where(mask, -_softplus(s), 0.0)
            hi = lk.astype(BF)
            lo = (lk - hi.astype(F32)).astype(BF)
            suf = _dot(hi, u_ref[...]) + _dot(lo, u_ref[...])
            w = jnp.where(mask, jnp.exp(s + lk + suf), 0.0)
            acc_s[bi] += _dot_t(w.astype(BF), vb)
            m_s[bi] = jnp.broadcast_to(jnp.sum(lk, axis=1, keepdims=True), (R, LANES))
        else:
            mask = (key <= t) & (key < n_new)
            if mode == "sel":
                mask = mask & (selnew_ref[bi] > 0.5)
            if mode == "fox":
                s = s + cq_ref[bi][:, :1] - cknew_ref[bi]
            _dec_softmax_step(jnp.where(mask, s, -jnp.inf), [vb], m_s.at[bi], l_s.at[bi], acc_s.at[bi])

    def past_pages(bi):
        q = q_ref[bi]
        kvs = [split_kv(p[...]) for p in pages[bi]]
        zs = [_dot(q, kb) for kb, _ in kvs]
        if mode == "sb":
            u = u_ref[...]
            carry = m_s[bi][:, :1]
            pv = None
            for z, (_, vb) in zip(zs, kvs):
                lk = -_softplus(z)
                hi = lk.astype(BF)
                lo = (lk - hi.astype(F32)).astype(BF)
                suf = _dot(hi, u) + _dot(lo, u)
                w = jnp.exp(z + lk + suf + carry)
                d = _dot_t(w.astype(BF), vb)
                pv = d if pv is None else pv + d
                carry = carry + jnp.sum(lk, axis=1, keepdims=True)
            acc_s[bi] += pv
            m_s[bi] = jnp.broadcast_to(carry, (R, LANES))
        else:
            s = jnp.concatenate(zs, axis=1) if G > 1 else zs[0]
            if mode == "fox":
                s = s + cq_ref[bi][:, :1] - ck_ref[bi]
            if mode == "sel":
                s = jnp.where(_dot(selx_ref[bi], e2_ref[...]) > 0.5, s, -jnp.inf)
            if mode == "win":
                widx = c * (G * PAGE) + lax.broadcasted_iota(jnp.int32, s.shape, 1)
                s = jnp.where(widx > row_t(s.shape) + (wlen - window), s, -jnp.inf)
            _dec_softmax_step(s, [vb for _, vb in kvs], m_s.at[bi], l_s.at[bi], acc_s.at[bi])

    @pl.when(c == 0)
    def _():
        m_s[...] = jnp.full(m_s.shape, 0.0 if mode == "sb" else -jnp.inf, F32)
        l_s[...] = jnp.zeros_like(l_s)
        acc_s[...] = jnp.zeros_like(acc_s)
        for bi in range(bb):
            new_rows(bi)

    for bi in range(bb):
        past_pages(bi)

    @pl.when(c == nc - 1)
    def _():
        if mode == "sb":
            o_ref[...] = acc_s[...]
        else:
            l = l_s[...][:, :, :1]
            o_ref[...] = acc_s[...] / jnp.where(l > 0.0, l, 1.0)


def dec_attn(pt, q, cache, new, *, mode, extras=(), n_new, rows_per_t, t_period, window=NSA_WINDOW):
    B, n_pages = pt.shape
    _, R, Dq = q.shape
    W = cache.shape[1]
    G = _pick_tile(n_pages, DEC_PAGES_PER_STEP)
    bb = _pick_tile(B, DEC_SEQS_PER_STEP)
    nc = n_pages // G
    dv = 256 if mode == "mla" else 128
    if mode == "sb":
        page_idx = lambda b, c, pt, bi, p: (pt[b * bb + bi, n_pages - 1 - (c * G + p)], 0, 0)
    elif mode == "win":
        page_idx = lambda b, c, pt, bi, p: (b * bb + bi, 0, c * G + p)
    else:
        page_idx = lambda b, c, pt, bi, p: (pt[b * bb + bi, c * G + p], 0, 0)
    per_b = lambda shape: pl.BlockSpec((bb,) + shape, lambda b, c, pt: (b,) + (0,) * len(shape))
    in_specs = [per_b((R, Dq))]
    if mode == "fox":
        in_specs += [per_b((R, LANES)), pl.BlockSpec((bb, R, G * PAGE), lambda b, c, pt: (b, 0, c)), per_b((R, NEW_PAD))]
    if mode == "sel":
        in_specs += [pl.BlockSpec((bb, None, R, 2 * G), lambda b, c, pt: (b, c, 0, 0)),
                     pl.BlockSpec((2 * G, G * PAGE), lambda b, c, pt: (0, 0)), per_b((R, NEW_PAD))]
    if mode == "sb":
        in_specs += [pl.BlockSpec((PAGE, PAGE), lambda b, c, pt: (0, 0))]
    in_specs += [pl.BlockSpec((None, W, PAGE), functools.partial(page_idx, bi=bi, p=p)) for bi in range(bb) for p in range(G)]
    in_specs += [per_b((W, NEW_PAD))]
    return pl.pallas_call(
        functools.partial(_dec_body, mode=mode, G=G, bb=bb, nc=nc, n_new=n_new, rows_per_t=rows_per_t, t_period=t_period,
                          wlen=n_pages * PAGE, window=window),
        grid_spec=pltpu.PrefetchScalarGridSpec(
            num_scalar_prefetch=1,
            grid=(B // bb, nc),
            in_specs=in_specs,
            out_specs=per_b((R, dv)),
            scratch_shapes=[pltpu.VMEM((bb, R, LANES), F32), pltpu.VMEM((bb, R, LANES), F32), pltpu.VMEM((bb, R, dv), F32)],
        ),
        out_shape=jax.ShapeDtypeStruct((B, R, dv), F32),
        compiler_params=_cparams(("parallel", "arbitrary")),
        name="dec_" + mode,
    )(pt, q, *extras, *([cache] * (bb * G)), new)


def _page_rows_body(pt_ref, *refs, G):
    del pt_ref
    o_ref = refs[G]
    o_ref[...] = jnp.concatenate([r[...] for r in refs[:G]], axis=0)


def gather_page_rows(pt, cache_rows):
    B, n_pages = pt.shape
    _, H, W = cache_rows.shape
    G = _pick_tile(n_pages, 32)
    return pl.pallas_call(
        functools.partial(_page_rows_body, G=G),
        grid_spec=pltpu.PrefetchScalarGridSpec(
            num_scalar_prefetch=1,
            grid=(B, n_pages // G),
            in_specs=[pl.BlockSpec((None, H, W), lambda b, c, pt, p=p: (pt[b, c * G + p], 0, 0)) for p in range(G)],
            out_specs=pl.BlockSpec((None, G * H, W), lambda b, c, pt: (b, c, 0)),
        ),
        out_shape=jax.ShapeDtypeStruct((B, n_pages * H, W), cache_rows.dtype),
        compiler_params=_cparams(("parallel", "parallel")),
        name="gather_page_rows",
    )(pt, *([cache_rows] * G))


def _cmp_pages_body(pt_ref, *refs, G, nc, n_pages):
    del pt_ref
    pages = refs[:G]
    w_ref, b_ref, o_ref, x_s = refs[G:]
    c = pl.program_id(1)
    F = pages[0].shape[0]
    for p in range(G):
        x_s[pl.ds(pl.multiple_of((c * G + p) * F, F), F), :] = pages[p][...]

    @pl.when(c == nc - 1)
    def _():
        for kv in range(2):
            for g in range(NSA_GROUPS):
                f0 = (kv * NSA_GROUPS + g) * HEAD_DIM
                x = jnp.concatenate([x_s[pl.ds(f0 + d, n_pages, stride=F), :] for d in range(HEAD_DIM)], axis=1)
                col = (kv * NSA_GROUPS + g) * LANES
                o_ref[:, col:col + LANES] = _dot(x.astype(BF), w_ref[kv]) + b_ref[:, col:col + LANES]


def compress_pages(pt, cache_t, w2, bias):
    B, n_pages = pt.shape
    F = cache_t.shape[1]
    G = _pick_tile(n_pages, 16)
    nc = n_pages // G
    return pl.pallas_call(
        functools.partial(_cmp_pages_body, G=G, nc=nc, n_pages=n_pages),
        grid_spec=pltpu.PrefetchScalarGridSpec(
            num_scalar_prefetch=1,
            grid=(B, nc),
            in_specs=[pl.BlockSpec((None, F, PAGE), lambda b, c, pt, p=p: (pt[b, c * G + p], 0, 0)) for p in range(G)]
            + [pl.BlockSpec(w2.shape, lambda b, c, pt: (0, 0, 0)), pl.BlockSpec(bias.shape, lambda b, c, pt: (0, 0))],
            out_specs=pl.BlockSpec((None, n_pages, 4 * LANES), lambda b, c, pt: (b, 0, 0)),
            scratch_shapes=[pltpu.VMEM((n_pages * F, PAGE), F32)],
        ),
        out_shape=jax.ShapeDtypeStruct((B, n_pages, 4 * LANES), F32),
        compiler_params=_cparams(("parallel", "arbitrary")),
        name="compress_pages",
    )(pt, *([cache_t] * G), w2, bias)


def _page_prefix_body(within_ref, tot_ref, l_ref, o_ref):
    a, b, c = _split3(tot_ref[...])
    l = l_ref[...]
    o_ref[...] = within_ref[...] + _dot(l, a) + _dot(l, b) + _dot(l, c)


def page_prefix(y, lstrict):
    B, npg, W2 = y.shape
    W = W2 // 2
    return pl.pallas_call(
        _page_prefix_body,
        grid=(B,),
        in_specs=[pl.BlockSpec((None, npg, W), lambda b: (b, 0, 0)), pl.BlockSpec((None, npg, W), lambda b: (b, 0, 1)),
                  pl.BlockSpec((npg, npg), lambda b: (0, 0))],
        out_specs=pl.BlockSpec((None, npg, W), lambda b: (b, 0, 0)),
        out_shape=jax.ShapeDtypeStruct((B, npg, W), F32),
        compiler_params=_cparams(("parallel",)),
        name="page_prefix",
    )(y, y, lstrict)


def _swap_halves(w, hd):
    K, N = w.shape
    return w.reshape(K, N // hd, 2, hd // 2)[:, :, ::-1, :].reshape(K, N)


def _pad_last(a, n):
    return jnp.pad(a, [(0, 0)] * (a.ndim - 1) + [(0, n - a.shape[-1])])


def _rope_tabs(pos, d):
    inv = ROPE_THETA ** (-jnp.arange(0, d, 2, dtype=F32) / d)
    ang = pos.astype(F32)[:, None] * inv[None, :]
    c, s = jnp.cos(ang), jnp.sin(ang)
    return jnp.concatenate([c, c], axis=1), jnp.concatenate([-s, s], axis=1)


def _prep_weights(W):
    D = W["ev_w_in"].shape[1]
    P = {}
    w = W["ev_w_in"][0]
    kr = w[:, 1280:1312]
    P["ev_wext"] = jnp.concatenate([w[:, :1280], _pad_last(kr, 128), _pad_last(_swap_halves(kr, 32), 128)], axis=1).astype(BF)
    wuq = W["ev_w_uq"][0]
    wn, wr = wuq[:, :, :MLA_NOPE].reshape(256, 512), wuq[:, :, MLA_NOPE:].reshape(256, 256)
    P["ev_wuq"] = jnp.concatenate([wn, wr, _swap_halves(wr, MLA_ROPE)], axis=1).astype(BF)
    wuk, wuv = W["ev_w_uk"][0], W["ev_w_uv"][0]
    eye8 = jnp.eye(MLA_HEADS, dtype=F32)
    P["ev_wuk"] = wuk.reshape(256, 512).astype(BF)
    P["ev_wuv"] = wuv.reshape(256, 512).astype(BF)
    P["ev_wabs"] = jnp.einsum("chn,hH->hnHc", wuk, eye8).reshape(512, 2048).astype(BF)
    P["ev_wuv_bd"] = jnp.einsum("chv,hH->hcHv", wuv, eye8).reshape(2048, 512).astype(BF)
    P["ev_wout"] = W["ev_w_out"][0].astype(BF)
    P["ev_gq"] = W["ev_q_norm"][0][None, :]
    P["ev_gkv"] = W["ev_kv_norm"][0][None, :]

    w = W["od_w_in"][0]
    seg = lambda a, b: w[:, a:b]
    nq, ck, cv, sk, sv, wk, wv = seg(0, 512), seg(512, 640), seg(640, 768), seg(768, 896), seg(896, 1024), seg(1024, 1152), seg(1152, 1280)
    gl, fq, fk, fv, fl = seg(1280, 1304), seg(1304, 1816), seg(1816, 1944), seg(1944, 2072), seg(2072, 2080)
    sw = lambda a: _swap_halves(a, HEAD_DIM)
    P["od_wext"] = jnp.concatenate([nq, sw(nq), ck, sw(ck), cv, sk, sw(sk), sv, wk, sw(wk), wv, _pad_last(gl, 128), fq, fk, fv,
                                    _pad_last(fl, 128)], axis=1).astype(BF)
    P["od_fb"] = _pad_last(W["od_fox_fb"][0][None, :], 128)
    P["od_wout"] = W["od_w_out"][0].astype(BF)
    eye2 = jnp.eye(2, dtype=F32)
    wbig = jnp.einsum("klde,kK,gG->lkgdKGe", W["od_cmp_w"][0], eye2, eye2).reshape(NSA_BLOCK * 256, 256)
    pe_flat = jnp.broadcast_to(W["od_cmp_pe"][0].transpose(1, 0, 2)[:, :, None, :], (NSA_BLOCK, 2, 2, HEAD_DIM)).reshape(1, -1)
    P["cmp_w"] = wbig.astype(BF)
    P["cmp_b"] = linear(jnp.pad(pe_flat, ((0, 7), (0, 0))), wbig, mode="hi", tn=128, name="cmp_bias")[:1]
    P["cmp_w2"] = jnp.einsum("klde,hH->kdhlHe", W["od_cmp_w"][0], eye2).reshape(2, HEAD_DIM * PAGE, 2 * HEAD_DIM).astype(BF)
    P["cmp_b2"] = jnp.broadcast_to(P["cmp_b"].reshape(2, 2, 1, HEAD_DIM), (2, 2, 2, HEAD_DIM)).reshape(1, 8 * HEAD_DIM)
    r = np.arange(128)
    cidx = np.arange(1536)
    P["gate_expand"] = jnp.asarray(((r[:, None] // 3 == (cidx[None, :] % 512) // 64) & (r[:, None] % 3 == cidx[None, :] // 512)
                                    & (r[:, None] < 24)).astype(np.float32), dtype=BF)
    i = np.arange(PAGE)
    P["scan_w"] = jnp.asarray(np.concatenate([i[:, None] <= i[None, :], np.ones((PAGE, PAGE), bool)], axis=1).astype(np.float32),
                              dtype=BF)
    P["moe_wg"] = W["moe_w_gate"].astype(BF)
    P["moe_wu"] = W["moe_w_up"].astype(BF)
    P["moe_wd"] = W["moe_w_down"].astype(BF)
    return P


def _mods(c, W, l, T, per_token):
    B, D = c.shape
    m = linear(c, W["ada_w"], W["ada_b"].reshape(W["ada_b"].shape[0], 1, -1), widx=l, act="silu", name="ada_mod").reshape(B, 6, D)
    if per_token:
        return [jnp.repeat(m[:, j], T, axis=0)[None] for j in range(6)]
    return [m[:, j][:, None, :] for j in range(6)]


def _heads_major(a, B, T, G, hg):
    return a.reshape(B, T, G, hg, HEAD_DIM).transpose(0, 2, 3, 1, 4)


def _kv_major(a, B, T, G):
    return a.reshape(B, T, G, HEAD_DIM).transpose(0, 2, 1, 3)


def _from_heads_major(o, N):
    B, G, hg, T, d = o.shape
    return o.transpose(0, 3, 1, 2, 4).reshape(N, G * hg * d)


def _block_diag_q(q, B, T):
    q5 = q.reshape(B, T, 2, 4, HEAD_DIM).transpose(0, 2, 1, 3, 4).reshape(B, 2, T * 4, 1, HEAD_DIM)
    eye = jnp.eye(2, dtype=q.dtype)[None, :, None, :, None]
    return (q5 * eye).reshape(B, 2 * T * 4, 2 * HEAD_DIM)


def _pick_diag(o, B, T):
    o6 = o.reshape(B, 2, T, 4, 2, HEAD_DIM)
    return jnp.stack([o6[:, 0, :, :, 0], o6[:, 1, :, :, 1]], axis=2).reshape(B * T, 512)


def _pad_new(rows, B, T):
    return jnp.pad(rows.reshape(B, T, -1), ((0, 0), (0, NEW_PAD - T), (0, 0))).transpose(0, 2, 1)


def _token_minor(cache):
    n, t = cache.shape[:2]
    return jnp.moveaxis(cache, 1, -1).reshape(n, -1, t)


def _head_rows(T):
    kv, t, h = np.meshgrid(np.arange(2), np.arange(T), np.arange(4), indexing="ij")
    return (kv * 4 + h).reshape(-1), t.reshape(-1)


def _scan_logf(x_rows, P):
    B, rows, _ = x_rows.shape
    npg = rows // 8
    y = linear(x_rows.reshape(B * rows, PAGE), P["scan_w"], mode="split3", tm=2048, tn=2 * PAGE, name="scan_within")
    r = np.arange(rows)
    earlier = jnp.asarray(((r[:, None] // 8 > r[None, :] // 8) & (r[:, None] % 8 == r[None, :] % 8)).astype(np.float32), dtype=BF)
    cs = page_prefix(y.reshape(B, rows, 2 * PAGE), earlier)
    return cs.reshape(B, npg, 8, PAGE).transpose(0, 2, 1, 3).reshape(B, 8, npg * PAGE)


def _trunk(x, c, W, P, past):
    B, T, D = x.shape
    N = B * T
    sample = past is not None
    x2 = x.reshape(N, D)
    tm = N if sample else _pick_tile(T, 256)
    mod_rows = N if sample else T
    kw = dict(tm=tm, mod_rows=mod_rows)
    kw_moe = dict(tm=N if sample else _pick_tile(T, MOE_TOKEN_TILE), mod_rows=mod_rows)
    pos0 = past["len"] if sample else 0
    pos = pos0 + jnp.arange(T, dtype=jnp.int32)
    rep = (lambda a: jnp.tile(a, (B, 1))) if sample else (lambda a: a)
    tab_rows = N if sample else T
    out = {}

    sh_a, sc_a, gt_a, sh_m, sc_m, gt_m = _mods(c, W, 0, T, sample)
    c32, s32 = _rope_tabs(pos, MLA_ROPE)
    tabs = [rep(_pad_last(c32, 128)), rep(_pad_last(s32, 128)), rep(jnp.tile(c32, (1, 8))), rep(jnp.tile(s32, (1, 8)))]
    g0 = W["norm_mix"][0][None, :]
    if not sample:
        sbq, sbrows, mla, qm, km, vm, sbkv = even_front(x2, g0, sc_a, sh_a, P["ev_wext"], tabs, P["ev_gq"], P["ev_gkv"],
                                                         P["ev_wuq"], P["ev_wuk"], P["ev_wuv"], absorbed=False,
                                                         tab_rows=tab_rows, **kw)
        o_sb = gflash(sbq.reshape(B, T, -1), sbkv.reshape(B, T, -1), mode="sb", **ATT_TILES).reshape(N, 512)
        o_mla = pflash(qm[:, :, None], km, vm, mode="causal", **MLA_TILES)
        mix = jnp.concatenate([o_sb, _from_heads_major(o_mla, N)], axis=1)
    else:
        pt = past["pt"]
        sbq, sbrows, mla, qlat, qr = even_front(x2, g0, sc_a, sh_a, P["ev_wext"], tabs, P["ev_gq"], P["ev_gkv"], P["ev_wuq"],
                                                P["ev_wabs"], None, absorbed=True, tab_rows=tab_rows, **kw)
        u = jnp.asarray(np.tril(np.ones((PAGE, PAGE), np.float32), -1), dtype=BF)
        n_pool = past["sb"].shape[0]
        o = dec_attn(pt, _block_diag_q(sbq, B, T), _token_minor(past["sb"]), _pad_new(sbrows, B, T), mode="sb",
                     extras=(u,), n_new=T, rows_per_t=4, t_period=4 * T)
        o_sb = _pick_diag(o, B, T).astype(BF)
        qcat = jnp.concatenate([qlat.reshape(B, T, 8, 256), qr.reshape(B, T, 8, 32)], axis=-1).reshape(B, T * 8, 288)
        o_lat = dec_attn(pt, qcat, _token_minor(past["mla"]), _pad_new(mla, B, T), mode="mla", n_new=T,
                         rows_per_t=8, t_period=8 * T)
        o_mla = linear(o_lat.reshape(N, 2048), P["ev_wuv_bd"], out_dtype=BF, name="mla_v_up")
        mix = jnp.concatenate([o_sb, o_mla], axis=1)
    out["sb"] = sbrows.reshape(1, B, T, 2, SB_KV_HEADS, HEAD_DIM)
    out["mla"] = mla.reshape(1, B, T, 288)
    x2 = out_proj_res(mix, P["ev_wout"], x2, gt_a, **kw)
    x2 = moe_layer(x2, W["norm_ffn"][0][None, :], sc_m, sh_m, gt_m, P["wr"][0], P["br"][0],
                   P["moe_wg"], P["moe_wu"], P["moe_wd"], 0, **kw_moe)

    sh_a, sc_a, gt_a, sh_m, sc_m, gt_m = _mods(c, W, 1, T, sample)
    c64, s64 = _rope_tabs(pos, HEAD_DIM)
    tabs = [rep(jnp.tile(c64, (1, 8))), rep(jnp.tile(s64, (1, 8))), rep(jnp.tile(c64, (1, 2))), rep(jnp.tile(s64, (1, 2)))]
    nq, nqf, cmp_rows, slc_rows, win_rows, gates, fq, fox_rows, logf, *kvb = odd_front(
        x2, W["norm_mix"][1][None, :], sc_a, sh_a, P["od_wext"], tabs, P["od_fb"], tab_rows=tab_rows, packed=not sample, **kw)
    logf8 = logf[:, :8]
    G, hg = NSA_GROUPS, NSA_HEADS // NSA_GROUPS
    if not sample:
        nb = T // NSA_BLOCK
        nbp = _round_up(nb, LANES)
        kcv = linear(cmp_rows.astype(BF).reshape(B * nb, NSA_BLOCK * 256), P["cmp_w"], P["cmp_b"], tm=128, tn=256, name="compress")
        kcv = kcv.reshape(B, nb, 2, G, HEAD_DIM).transpose(2, 0, 3, 1, 4)
        kcv = jnp.pad(kcv, ((0, 0), (0, 0), (0, 0), (0, nbp - nb), (0, 0)))
        tq = _pick_tile(T, 128)
        nqt = T // tq
        qf = nqf.reshape(B, nqt, tq, G, hg, HEAD_DIM).transpose(0, 3, 1, 4, 2, 5).reshape(B, G, nqt, hg * tq, HEAD_DIM)
        oc, sel = nsa_cmp_sel(qf, kcv[0], kcv[1], tq=tq, nbc=nb, nbs=nb, pos0=0, bb=2)
        oc = oc.reshape(B, G, nqt, hg, tq, HEAD_DIM).transpose(0, 2, 4, 1, 3, 5).reshape(N, 512)
        sel = sel.reshape(B, G, T, nbp)
        expand = jnp.asarray((np.arange(nbp)[:, None] == np.arange(T)[None, :] // NSA_BLOCK).astype(np.float32), dtype=BF)
        slb, wnb, fxb = (a.reshape(B, T, -1) for a in kvb)
        nq3 = nq.reshape(B, T, -1)
        osel = gflash(nq3, slb, mode="sel", extras=(sel, expand), **ATT_TILES).reshape(N, 512)
        ow = gflash(nq3, wnb, mode="win", **ATT_TILES).reshape(N, 512)
        cs = _scan_logf(logf8.reshape(B, T // PAGE, PAGE, 8).transpose(0, 1, 3, 2).reshape(B, T // PAGE * 8, PAGE), P)
        ch = (cs * LOG2E).reshape(B, 2, 4, T)
        cq = jnp.broadcast_to(ch[..., None], (B, 2, 4, T, LANES))
        of = gflash(fq.reshape(B, T, -1), fxb, mode="fox", extras=(cq, ch[:, :, :, None, :]), **ATT_TILES).reshape(N, 512)
        wb = min(NSA_WINDOW, T)
        out["win"] = win_rows.reshape(1, B, T, 2, G, HEAD_DIM)[:, :, T - wb:]
    else:
        n_pages = pt.shape[1]
        L = n_pages * PAGE
        nbc = L // NSA_BLOCK
        nbs = nbc + 1
        nbp = _round_up(nbs, LANES)
        kcv = compress_pages(pt, _token_minor(past["cmp"]), P["cmp_w2"], P["cmp_b2"])
        kcv = kcv.reshape(B, n_pages, 2, G, 2, HEAD_DIM).transpose(2, 0, 3, 1, 4, 5).reshape(2, B, G, nbc, HEAD_DIM)
        kcv = jnp.pad(kcv, ((0, 0), (0, 0), (0, 0), (0, nbp - nbc), (0, 0)))
        qf = nqf.reshape(B, T, G, hg, HEAD_DIM).transpose(0, 2, 3, 1, 4).reshape(B, G, 1, hg * T, HEAD_DIM)
        oc, sel = nsa_cmp_sel(qf, kcv[0], kcv[1], tq=T, nbc=nbc, nbs=nbs, pos0=pos0, bb=4)
        oc = oc.reshape(B, G, hg, T, HEAD_DIM).transpose(0, 3, 1, 2, 4).reshape(N, 512)
        sel = sel.reshape(B, G, T, nbp)
        Gp = _pick_tile(n_pages, DEC_PAGES_PER_STEP)
        nc = n_pages // Gp
        selr = jnp.broadcast_to(sel[:, :, :, None, :], (B, G, T, hg, nbp)).reshape(B, G * T * hg, nbp)
        selx = selr[:, :, :2 * n_pages].reshape(B, G * T * hg, nc, 2 * Gp).transpose(0, 2, 1, 3)
        selnew = jnp.broadcast_to(selr[:, :, 2 * n_pages:2 * n_pages + 1].astype(F32), (B, G * T * hg, NEW_PAD))
        e2 = jnp.asarray((np.arange(2 * Gp)[:, None] == np.arange(Gp * PAGE)[None, :] // NSA_BLOCK).astype(np.float32), dtype=BF)
        rk = dict(n_new=T, rows_per_t=4, t_period=4 * T)
        qbd = _block_diag_q(nq, B, T)
        osel = dec_attn(pt, qbd, _token_minor(past["slc"]), _pad_new(slc_rows, B, T), mode="sel",
                        extras=(selx, e2, selnew), **rk)
        win_buf = past["win"]
        npw = win_buf.shape[1] // PAGE
        pt_win = jnp.zeros((B, npw), jnp.int32)
        ow = dec_attn(pt_win, qbd, _token_minor(win_buf), _pad_new(win_rows, B, T), mode="win", **rk)
        cpast = _scan_logf(gather_page_rows(pt, _token_minor(past["logf"])), P)
        lnew = logf8.reshape(B, T, 8)
        run = cpast[:, :, -1]
        cnew = []
        for tt in range(T):
            run = run + lnew[:, tt, :]
            cnew.append(run)
        cnew = jnp.stack(cnew, axis=1)
        hrow, trow = _head_rows(T)
        ck32 = cpast[:, hrow, :]
        cq32 = jnp.broadcast_to(cnew[:, trow, hrow][:, :, None], (B, len(hrow), LANES))
        cknew = _pad_last(cnew.transpose(0, 2, 1)[:, hrow, :], NEW_PAD)
        of = dec_attn(pt, _block_diag_q(fq, B, T), _token_minor(past["fox"]), _pad_new(fox_rows, B, T), mode="fox",
                      extras=(cq32, ck32, cknew), **rk)
        osel, ow, of = (_pick_diag(a, B, T).astype(BF) for a in (osel, ow, of))
        out["win"] = jnp.concatenate([win_buf, win_rows.reshape(B, T, 2, G, HEAD_DIM)], axis=1)[None, :, T:]
    mix = nsa_combine(oc, osel, ow, gates, P["gate_expand"], of, tm=tm)
    out["cmp"] = cmp_rows.reshape(1, B, T, 2, G, HEAD_DIM)
    out["slc"] = slc_rows.reshape(1, B, T, 2, G, HEAD_DIM)
    out["fox"] = fox_rows.reshape(1, B, T, 2, FOX_KV_HEADS, HEAD_DIM)
    out["logf"] = logf8.reshape(1, B, T, FOX_HEADS)
    x2 = out_proj_res(mix, P["od_wout"], x2, gt_a, **kw)
    x2 = moe_layer(x2, W["norm_ffn"][1][None, :], sc_m, sh_m, gt_m, P["wr"][1], P["br"][1],
                   P["moe_wg"], P["moe_wu"], P["moe_wd"], 1, **kw_moe)
    y = final_norm(x2, W["norm_final"][None, :], tm=tm).reshape(B, T, D)
    return y, out


def kernel(x_prompt, x_sample, c_prompt, c_sample, cache_sb_kv, cache_mla_latent, cache_nsa_cmp_kv, cache_nsa_slc_kv,
           cache_fox_kv, cache_fox_logf, state_nsa_win_kv, page_table, ada_w, ada_b, norm_mix, norm_ffn, norm_final,
           ev_w_in, ev_q_norm, ev_kv_norm, ev_w_uq, ev_w_uk, ev_w_uv, ev_w_out, od_w_in, od_cmp_pe, od_cmp_w, od_fox_fb,
           od_w_out, moe_w_grp, moe_b_grp, moe_w_rt, moe_b_rt, moe_w_gate, moe_w_up, moe_w_down):
    assert ada_w.shape[0] == 2, "two trunk layers (one even, one odd) are implemented"
    W = dict(ada_w=ada_w, ada_b=ada_b, norm_mix=norm_mix, norm_ffn=norm_ffn, norm_final=norm_final, ev_w_in=ev_w_in,
             ev_q_norm=ev_q_norm, ev_kv_norm=ev_kv_norm, ev_w_uq=ev_w_uq, ev_w_uk=ev_w_uk, ev_w_uv=ev_w_uv, ev_w_out=ev_w_out,
             od_w_in=od_w_in, od_cmp_pe=od_cmp_pe, od_cmp_w=od_cmp_w, od_fox_fb=od_fox_fb, od_w_out=od_w_out,
             moe_w_gate=moe_w_gate, moe_w_up=moe_w_up, moe_w_down=moe_w_down)
    P = _prep_weights(W)
    P["wr"] = _pad_last(jnp.concatenate([moe_w_rt, moe_w_grp], axis=-1), LANES)
    P["br"] = _pad_last(jnp.concatenate([moe_b_rt, moe_b_grp], axis=-1), LANES)[:, None, :]
    past = dict(pt=page_table, len=page_table.shape[1] * cache_sb_kv.shape[2], sb=cache_sb_kv[0], mla=cache_mla_latent[0],
                cmp=cache_nsa_cmp_kv[0], slc=cache_nsa_slc_kv[0], fox=cache_fox_kv[0], logf=cache_fox_logf[0],
                win=state_nsa_win_kv[0])
    y_p, sp = _trunk(x_prompt, c_prompt, W, P, None)
    y_s, ss = _trunk(x_sample, c_sample, W, P, past)
    names = ("sb", "mla", "cmp", "slc", "fox", "logf", "win")
    outs = [y_p, y_s]
    for n in names:
        outs += [sp[n], ss[n]]
    return tuple(outs)
```

```python
import functools
import math

import jax
import jax.numpy as jnp
import numpy as np
from jax import lax
from jax.experimental import pallas as pl
from jax.experimental.pallas import tpu as pltpu

F32 = jnp.float32
BF = jnp.bfloat16
HI = lax.Precision.HIGHEST

HEAD_DIM = 64
ROPE_THETA = 10000.0
RMS_EPS = 1e-6
SB_HEADS, SB_KV_HEADS = 8, 2
MLA_HEADS, MLA_NOPE, MLA_ROPE, MLA_V = 8, 64, 32, 64
NSA_HEADS, NSA_GROUPS, NSA_BLOCK, NSA_TOPN, NSA_WINDOW = 8, 2, 64, 16, 512
FOX_HEADS, FOX_KV_HEADS = 8, 2
MOE_GROUPS, MOE_PER_GROUP, MOE_TOPK = 4, 8, 2
MOE_EXPERTS = MOE_GROUPS * MOE_PER_GROUP
MLA_SCALE = 1.0 / math.sqrt(MLA_NOPE + MLA_ROPE)
QK_SCALE = 1.0 / math.sqrt(HEAD_DIM)
LOG2E = math.log2(math.e)
LANES = 128
VMEM_LIMIT_MB = 56
ATT_TILES = dict(tq=256, tk=256)
FOX_TILES = dict(tq=256, tk=512)
MLA_TILES = dict(tq=512, tk=512, gb=4)
MOE_TOKEN_TILE = 1024
MOE_EXPERTS_PER_STEP = 4


def _cparams(sem, vmem_mb=VMEM_LIMIT_MB):
    return pltpu.CompilerParams(dimension_semantics=sem, vmem_limit_bytes=vmem_mb * 2 ** 20)


def _round_up(a, m):
    return (a + m - 1) // m * m


def _pick_tile(n, pref):
    t = min(pref, n)
    while n % t:
        t //= 2
    return t


def _rms(x, g):
    return x * lax.rsqrt(jnp.mean(x * x, axis=-1, keepdims=True) + RMS_EPS) * g


def _dot(a, b):
    return jnp.dot(a, b, preferred_element_type=F32)


def _dot_t(a, b, precision=None):
    return lax.dot_general(a, b, (((1,), (1,)), ((), ())), preferred_element_type=F32, precision=precision)


def _split3(x):
    a = x.astype(BF)
    r = x - a.astype(F32)
    b = r.astype(BF)
    c = (r - b.astype(F32)).astype(BF)
    return a, b, c


def _linear_body(*refs, act, mode, has_bias):
    x_ref, w_ref = refs[0], refs[1]
    b_ref = refs[2] if has_bias else None
    o_ref = refs[-1]
    x = x_ref[...]
    if act == "silu":
        x = x.astype(F32)
        x = x * jax.nn.sigmoid(x)
    w = w_ref[...]
    if mode == "hi":
        y = jnp.dot(x.astype(F32), w.astype(F32), precision=HI, preferred_element_type=F32)
    elif mode == "split3":
        a, b, c = _split3(x.astype(F32))
        wb = w.astype(BF)
        y = _dot(a, wb) + _dot(b, wb) + _dot(c, wb)
    else:
        y = _dot(x.astype(BF), w.astype(BF))
    if has_bias:
        y = y + b_ref[...]
    o_ref[...] = y.astype(o_ref.dtype)


def linear(x, w, bias=None, *, widx=None, act=None, mode="bf16", out_dtype=F32, tm=256, tn=512, name="linear"):
    M, K = x.shape
    N = w.shape[-1]
    tm = _pick_tile(M, tm)
    tn = _pick_tile(N, tn)
    if w.ndim == 3:
        w_spec = pl.BlockSpec((None, K, tn), lambda i, j: (widx, 0, j))
    else:
        w_spec = pl.BlockSpec((K, tn), lambda i, j: (0, j))
    in_specs = [pl.BlockSpec((tm, K), lambda i, j: (i, 0)), w_spec]
    args = [x, w]
    if bias is not None:
        if bias.ndim == 3:
            in_specs.append(pl.BlockSpec((None, 1, tn), lambda i, j: (widx, 0, j)))
        else:
            in_specs.append(pl.BlockSpec((1, tn), lambda i, j: (0, j)))
        args.append(bias)
    return pl.pallas_call(
        functools.partial(_linear_body, act=act, mode=mode, has_bias=bias is not None),
        grid=(M // tm, N // tn),
        in_specs=in_specs,
        out_specs=pl.BlockSpec((tm, tn), lambda i, j: (i, j)),
        out_shape=jax.ShapeDtypeStruct((M, N), out_dtype),
        compiler_params=_cparams(("parallel", "parallel")),
        name=name,
    )(*args)


def _mod_spec(tm, D, mod_rows):
    return lambda R: pl.BlockSpec((None, R, D), lambda i: ((i * tm) // mod_rows, 0, 0))


def _even_front_body(x_ref, g_ref, sc_ref, sh_ref, w_ref, c32_ref, s32_ref, c256_ref, s256_ref, gq_ref, gkv_ref,
                     wuq_ref, wa_ref, wb_ref, sbq_ref, sbrows_ref, mla_ref, o1_ref, o2_ref, o3_ref, o4_ref, *, absorbed):
    hn = _rms(x_ref[...], g_ref[...]) * (1.0 + sc_ref[...]) + sh_ref[...]
    proj = _dot(hn.astype(BF), w_ref[...])
    if absorbed:
        sbq_ref[...] = (proj[:, :512] * QK_SCALE).astype(BF)
    else:
        sbq_ref[...] = _pad_heads(proj[:, :512] * QK_SCALE).astype(BF)
        o4_ref[...] = proj[:, 512:768].astype(BF)
    sbrows_ref[...] = proj[:, 512:768]
    cqn = _rms(proj[:, 768:1024], gq_ref[...])
    ckvn = _rms(proj[:, 1024:1280], gkv_ref[...])
    krr = proj[:, 1280:1408] * c32_ref[...] + proj[:, 1408:1536] * s32_ref[...]
    mla_ref[:, :256] = ckvn
    mla_ref[:, 256:288] = krr[:, :32]
    qu = _dot(cqn.astype(BF), wuq_ref[...])
    qr = (qu[:, 512:768] * c256_ref[...] + qu[:, 768:1024] * s256_ref[...]) * MLA_SCALE
    qn = qu[:, :512]
    if absorbed:
        o1_ref[...] = (_dot(qn.astype(BF), wa_ref[...]) * MLA_SCALE).astype(BF)
        o2_ref[...] = qr.astype(BF)
    else:
        ckb = ckvn.astype(BF)
        kn = _dot(ckb, wa_ref[...])
        vv = _dot(ckb, wb_ref[...])
        qs = qn * (MLA_SCALE * LOG2E)
        qr2 = qr * LOG2E
        kr = krr[:, :MLA_ROPE]
        for h in range(MLA_HEADS):
            nope = slice(h * MLA_NOPE, (h + 1) * MLA_NOPE)
            o1_ref[h] = jnp.concatenate([qs[:, nope], qr2[:, h * MLA_ROPE:(h + 1) * MLA_ROPE]], axis=1).astype(BF)
            o2_ref[h] = jnp.concatenate([kn[:, nope], kr], axis=1).astype(BF)
            o3_ref[h] = vv[:, h * MLA_V:(h + 1) * MLA_V].astype(BF)


def even_front(x2, g, sc, sh, wext, tabs, gq, gkv, wuq, wa, wb, *, tm, mod_rows, tab_rows, absorbed):
    N, D = x2.shape
    R = sc.shape[1]
    c32, s32, c256, s256 = tabs
    nt = tab_rows // tm
    row = lambda w: pl.BlockSpec((tm, w), lambda i: (i, 0))
    tab = lambda w: pl.BlockSpec((tm, w), lambda i: (i % nt, 0))
    full = lambda a: pl.BlockSpec(a.shape, lambda i: (0,) * a.ndim)
    mod = _mod_spec(tm, D, mod_rows)(R)
    in_specs = [row(D), full(g), mod, mod, full(wext), tab(128), tab(128), tab(256), tab(256), full(gq), full(gkv),
                full(wuq), full(wa)]
    args = [x2, g, sc, sh, wext, c32, s32, c256, s256, gq, gkv, wuq, wa]
    if absorbed:
        body = lambda *r: _even_front_body(*r[:13], None, *r[13:], None, None, absorbed=True)
        outs = [(512, BF), (256, F32), (288, F32), (2048, BF), (256, BF)]
        out_specs = [row(w) for w, _ in outs]
        out_shape = [jax.ShapeDtypeStruct((N, w), dt) for w, dt in outs]
    else:
        in_specs.append(full(wb))
        args.append(wb)
        body = functools.partial(_even_front_body, absorbed=False)
        tps = mod_rows // tm
        B = N // mod_rows
        heads = lambda w: pl.BlockSpec((None, MLA_HEADS, tm, w), lambda i: (i // tps, 0, i % tps, 0))
        hshape = lambda w: jax.ShapeDtypeStruct((B, MLA_HEADS, mod_rows, w), BF)
        dk = MLA_NOPE + MLA_ROPE
        out_specs = [row(1024), row(256), row(288), heads(dk), heads(dk), heads(MLA_V), row(256)]
        out_shape = [jax.ShapeDtypeStruct((N, 1024), BF), jax.ShapeDtypeStruct((N, 256), F32), jax.ShapeDtypeStruct((N, 288), F32),
                     hshape(dk), hshape(dk), hshape(MLA_V), jax.ShapeDtypeStruct((N, 256), BF)]
    return pl.pallas_call(
        body,
        grid=(N // tm,),
        in_specs=in_specs,
        out_specs=out_specs,
        out_shape=out_shape,
        compiler_params=_cparams(("parallel",)),
        name="even_front",
    )(*args)


def _log_sigmoid(z):
    return jnp.minimum(z, 0.0) - jnp.log(1.0 + jnp.exp(-jnp.abs(z)))


def _odd_front_body(x_ref, g_ref, sc_ref, sh_ref, w_ref, c512_ref, s512_ref, c128_ref, s128_ref, fb_ref,
                    nq_ref, nqf_ref, cmp_ref, slc_ref, win_ref, gates_ref, fq_ref, fox_ref, logf_ref, *kvb_refs, packed):
    hn = _rms(x_ref[...], g_ref[...]) * (1.0 + sc_ref[...]) + sh_ref[...]
    p = _dot(hn.astype(BF), w_ref[...])
    nq = (p[:, 0:512] * c512_ref[...] + p[:, 512:1024] * s512_ref[...]) * QK_SCALE
    nq_ref[...] = (_pad_heads(nq * LOG2E) if packed else nq).astype(BF)
    nqf_ref[...] = nq
    c128, s128 = c128_ref[...], s128_ref[...]
    cmp_ref[:, :128] = p[:, 1024:1152] * c128 + p[:, 1152:1280] * s128
    cmp_ref[:, 128:] = p[:, 1280:1408]
    slc_ref[:, :128] = p[:, 1408:1536] * c128 + p[:, 1536:1664] * s128
    slc_ref[:, 128:] = p[:, 1664:1792]
    win_ref[:, :128] = p[:, 1792:1920] * c128 + p[:, 1920:2048] * s128
    win_ref[:, 128:] = p[:, 2048:2176]
    gates_ref[...] = jax.nn.sigmoid(p[:, 2176:2304])
    fq = p[:, 2304:2816] * QK_SCALE
    fq_ref[...] = (_pad_heads(fq * LOG2E) if packed else fq).astype(BF)
    fox_ref[...] = p[:, 2816:3072]
    logf_ref[...] = _log_sigmoid(p[:, 3072:3200] + fb_ref[...])
    if packed:
        for src, dst in zip((slc_ref, win_ref, fox_ref), kvb_refs):
            dst[...] = src[...].astype(BF)


def odd_front(x2, g, sc, sh, wext, tabs, fb, *, tm, mod_rows, tab_rows, packed):
    N, D = x2.shape
    R = sc.shape[1]
    nt = tab_rows // tm
    row = lambda w: pl.BlockSpec((tm, w), lambda i: (i, 0))
    tab = lambda w: pl.BlockSpec((tm, w), lambda i: (i % nt, 0))
    full = lambda a: pl.BlockSpec(a.shape, lambda i: (0,) * a.ndim)
    mod = _mod_spec(tm, D, mod_rows)(R)
    qw = 1024 if packed else 512
    outs = [(qw, BF), (512, F32), (256, F32), (256, F32), (256, F32), (128, F32), (qw, BF), (256, F32), (128, F32)]
    if packed:
        outs += [(256, BF)] * 3
    return pl.pallas_call(
        functools.partial(_odd_front_body, packed=packed),
        grid=(N // tm,),
        in_specs=[row(D), full(g), mod, mod, full(wext), tab(512), tab(512), tab(128), tab(128), full(fb)],
        out_specs=[row(w) for w, _ in outs],
        out_shape=[jax.ShapeDtypeStruct((N, w), dt) for w, dt in outs],
        compiler_params=_cparams(("parallel",)),
        name="odd_front",
    )(x2, g, sc, sh, wext, *tabs, fb)


def _outproj_body(mix_ref, w_ref, x_ref, gt_ref, o_ref):
    o_ref[...] = x_ref[...] + gt_ref[...] * _dot(mix_ref[...].astype(BF), w_ref[...])


def out_proj_res(mix, w, x2, gt, *, tm, mod_rows):
    N, D = x2.shape
    K = mix.shape[1]
    return pl.pallas_call(
        _outproj_body,
        grid=(N // tm,),
        in_specs=[pl.BlockSpec((tm, K), lambda i: (i, 0)), pl.BlockSpec(w.shape, lambda i: (0, 0)),
                  pl.BlockSpec((tm, D), lambda i: (i, 0)), _mod_spec(tm, D, mod_rows)(gt.shape[1])],
        out_specs=pl.BlockSpec((tm, D), lambda i: (i, 0)),
        out_shape=jax.ShapeDtypeStruct((N, D), F32),
        compiler_params=_cparams(("parallel",)),
        name="out_proj_res",
    )(mix, w, x2, gt)


def _nsa_combine_body(oc_ref, os_ref, ow_ref, g_ref, e_ref, of_ref, o_ref):
    a, b, c = _split3(g_ref[...])
    e = e_ref[...]
    gx = _dot(a, e) + _dot(b, e) + _dot(c, e)
    o = (gx[:, :512] * oc_ref[...].astype(F32) + gx[:, 512:1024] * os_ref[...].astype(F32)
         + gx[:, 1024:] * ow_ref[...].astype(F32))
    o_ref[:, :512] = o.astype(BF)
    o_ref[:, 512:] = of_ref[...]


def nsa_combine(oc, osel, ow, gates, expand, of, *, tm):
    N = oc.shape[0]
    row = lambda w: pl.BlockSpec((tm, w), lambda i: (i, 0))
    return pl.pallas_call(
        _nsa_combine_body,
        grid=(N // tm,),
        in_specs=[row(512), row(512), row(512), row(128), pl.BlockSpec(expand.shape, lambda i: (0, 0)), row(512)],
        out_specs=row(1024),
        out_shape=jax.ShapeDtypeStruct((N, 1024), BF),
        compiler_params=_cparams(("parallel",)),
        name="nsa_combine",
    )(oc, osel, ow, gates, expand, of)


def _final_norm_body(x_ref, g_ref, o_ref):
    o_ref[...] = _rms(x_ref[...], g_ref[...])


def final_norm(x2, g, *, tm):
    N, D = x2.shape
    return pl.pallas_call(
        _final_norm_body,
        grid=(N // tm,),
        in_specs=[pl.BlockSpec((tm, D), lambda i: (i, 0)), pl.BlockSpec((1, D), lambda i: (0, 0))],
        out_specs=pl.BlockSpec((tm, D), lambda i: (i, 0)),
        out_shape=jax.ShapeDtypeStruct((N, D), F32),
        compiler_params=_cparams(("parallel",)),
        name="final_norm",
    )(x2, g)


def _moe_body(x_ref, g_ref, sc_ref, sh_ref, gt_ref, wr_ref, br_ref, wg_ref, wu_ref, wd_ref, o_ref, hn_s, gates_s, acc_s):
    e = pl.program_id(1)
    lane = lax.broadcasted_iota(jnp.int32, gates_s.shape, 1)
    lanef = lane.astype(F32)

    @pl.when(e == 0)
    def _():
        hn = _rms(x_ref[...], g_ref[...]) * (1.0 + sc_ref[...]) + sh_ref[...]
        hn_s[...] = hn.astype(BF)
        h_hi = hn.astype(BF)
        h_lo = (hn - h_hi.astype(F32)).astype(BF)
        w = wr_ref[...]
        w_hi = w.astype(BF)
        w_lo = (w - w_hi.astype(F32)).astype(BF)
        lr = _dot(h_hi, w_hi) + _dot(h_lo, w_hi) + _dot(h_hi, w_lo) + br_ref[...]
        lg = jnp.where((lane >= MOE_EXPERTS) & (lane < MOE_EXPERTS + MOE_GROUPS), lr, -jnp.inf)
        mg = jnp.max(lg, axis=1, keepdims=True)
        g_sel = jnp.min(jnp.where(lg == mg, lanef, 1e9), axis=1, keepdims=True) - MOE_EXPERTS
        g_w = 1.0 / jnp.sum(jnp.exp(lg - mg), axis=1, keepdims=True)
        ingrp = (lane >> 3).astype(F32) == g_sel
        ingrp = ingrp & (lane < MOE_EXPERTS)
        lrm = jnp.where(ingrp, lr, -jnp.inf)
        m1 = jnp.max(lrm, axis=1, keepdims=True)
        ex = jnp.exp(lrm - m1)
        i1 = jnp.min(jnp.where(ingrp & (ex == 1.0), lanef, 1e9), axis=1, keepdims=True)
        c2 = jnp.where(ingrp & (lanef != i1), ex, -1.0)
        v2 = jnp.max(c2, axis=1, keepdims=True)
        i2 = jnp.min(jnp.where(c2 == v2, lanef, 1e9), axis=1, keepdims=True)
        den = 1.0 + v2
        gates_s[...] = jnp.where(lanef == i1, g_w / den, 0.0) + jnp.where(lanef == i2, g_w * v2 / den, 0.0)
        acc_s[...] = jnp.zeros_like(acc_s)

    hb = hn_s[...]
    gates = gates_s[...]
    hids = []
    for k in range(MOE_EXPERTS_PER_STEP):
        ge = jnp.sum(jnp.where(lane == e * MOE_EXPERTS_PER_STEP + k, gates, 0.0), axis=1, keepdims=True)
        a = _dot(hb, wg_ref[k])
        u = _dot(hb, wu_ref[k])
        hids.append((a * jax.nn.sigmoid(a) * u * ge).astype(BF))
    wd = wd_ref[...]
    acc_s[...] += _dot(jnp.concatenate(hids, axis=1), wd.reshape(wd.shape[0] * wd.shape[1], wd.shape[2]))

    @pl.when(e == pl.num_programs(1) - 1)
    def _():
        o_ref[...] = x_ref[...] + gt_ref[...] * acc_s[...]


def moe_layer(x2, g, sc, sh, gt, wr, br, wg, wu, wd, l, *, tm, mod_rows):
    N, D = x2.shape
    Hh = wg.shape[-1]
    row = pl.BlockSpec((tm, D), lambda i, e: (i, 0))
    mod = lambda a: pl.BlockSpec((None, a.shape[1], D), lambda i, e: ((i * tm) // mod_rows, 0, 0))
    full = lambda a: pl.BlockSpec(a.shape, lambda i, e: (0,) * a.ndim)
    return pl.pallas_call(
        _moe_body,
        grid=(N // tm, MOE_EXPERTS // MOE_EXPERTS_PER_STEP),
        in_specs=[row, full(g), mod(sc), mod(sh), mod(gt), full(wr), full(br),
                  pl.BlockSpec((None, MOE_EXPERTS_PER_STEP, D, Hh), lambda i, e: (l, e, 0, 0)),
                  pl.BlockSpec((None, MOE_EXPERTS_PER_STEP, D, Hh), lambda i, e: (l, e, 0, 0)),
                  pl.BlockSpec((None, MOE_EXPERTS_PER_STEP, Hh, D), lambda i, e: (l, e, 0, 0))],
        out_specs=row,
        out_shape=jax.ShapeDtypeStruct((N, D), F32),
        scratch_shapes=[pltpu.VMEM((tm, D), BF), pltpu.VMEM((tm, LANES), F32), pltpu.VMEM((tm, D), F32)],
        compiler_params=_cparams(("parallel", "arbitrary")),
        name="moe",
    )(x2, g, sc, sh, gt, wr, br, wg, wu, wd)


def _softplus(z):
    return jnp.maximum(z, 0.0) + jnp.log(1.0 + jnp.exp(-jnp.abs(z)))


def _pflash_body(*refs, mode, gb, hg, tq, tk, nk, dv, window):
    it = iter(refs)
    q_ref, k_ref, v_ref = next(it), next(it), next(it)
    cq_ref = ck_ref = sel_ref = e_ref = u_ref = None
    if mode == "fox":
        cq_ref, ck_ref = next(it), next(it)
    if mode == "sel":
        sel_ref, e_ref = next(it), next(it)
    if mode == "sb":
        u_ref = next(it)
    o_ref, m_s, acc_s = next(it), next(it), next(it)
    i, j = pl.program_id(2), pl.program_id(3)
    R = hg * tq
    last = (i * tq + tq - 1) // tk

    @pl.when(j == 0)
    def _():
        m_s[...] = jnp.full(m_s.shape, 0.0 if mode == "sb" else -jnp.inf, F32)
        acc_s[...] = jnp.zeros_like(acc_s)

    if mode == "win":
        kt = last - (nk - 1) + j
        active = kt >= 0
        edge = active
    elif mode == "sb":
        kt = last - j
        active = kt >= 0
        edge = j == 0
    else:
        kt = j
        active = j <= last
        edge = j == last

    def tile(gi, positional):
        q = q_ref[gi].reshape(R, q_ref.shape[-1])
        s3 = _dot_t(q, k_ref[gi]).reshape(hg, tq, tk)
        mask = None
        if positional:
            qpos = i * tq + lax.broadcasted_iota(jnp.int32, (tq, tk), 0)
            kpos = kt * tk + lax.broadcasted_iota(jnp.int32, (tq, tk), 1)
            if mode == "sb":
                mask = kpos < qpos
            elif mode == "win":
                d = qpos - kpos
                mask = (d >= 0) & (d < window)
            else:
                mask = kpos <= qpos
        if mode == "sel":
            picked = _dot(sel_ref[gi], e_ref[...]) > 0.5
            mask = picked if mask is None else mask & picked
        if mode == "fox":
            s3 = s3 - ck_ref[gi]
        chunk = lambda a: [a[:, c * LANES:(c + 1) * LANES] for c in range(tk // LANES)]
        if mode == "sb":
            lk3 = -_softplus(s3)
            if mask is not None:
                lk3 = jnp.where(mask[None], lk3, 0.0)
            lk = lk3.reshape(R, tk)
            hi = lk.astype(BF)
            lo = (lk - hi.astype(F32)).astype(BF)
            u = u_ref[...]
            sufx = _dot(hi, u) + _dot(lo, u)
            carry = m_s[gi]
            e = jnp.concatenate([a + b + carry for a, b in zip(chunk((s3 + lk3).reshape(R, tk)), chunk(sufx))], axis=1)
            w3 = jnp.exp(e.reshape(hg, tq, tk))
            if mask is not None:
                w3 = jnp.where(mask[None], w3, 0.0)
            acc_s[gi] += _dot(w3.reshape(R, tk).astype(BF), v_ref[gi])
            m_s[gi] = carry + sufx[:, tk:]
        else:
            if mask is not None:
                s3 = jnp.where(mask[None], s3, -jnp.inf)
            chunks = chunk(s3.reshape(R, tk))
            if mode == "fox":
                cq = cq_ref[gi].reshape(R, LANES)
                chunks = [a + cq for a in chunks]
            m_prev = m_s[gi]
            m_new = jnp.maximum(m_prev, jnp.max(functools.reduce(jnp.maximum, chunks), axis=1, keepdims=True))
            m_use = jnp.where(m_new > -jnp.inf, m_new, 0.0)
            p = jnp.concatenate([jnp.exp2(a - m_use) for a in chunks], axis=1).astype(BF)
            acc_s[gi] = jnp.exp2(m_prev - m_use) * acc_s[gi] + _dot(p, v_ref[gi])
            m_s[gi] = m_new

    @pl.when(active & edge)
    def _():
        for gi in range(gb):
            tile(gi, True)

    if mode != "win":
        @pl.when(active & jnp.logical_not(edge))
        def _():
            for gi in range(gb):
                tile(gi, False)

    @pl.when(j == nk - 1)
    def _():
        acc = acc_s[...]
        out = acc[:, :, :dv]
        if mode != "sb":
            l = acc[:, :, dv:dv + 1]
            out = out / jnp.where(l > 0.0, l, 1.0)
        o_ref[...] = out.reshape(o_ref.shape).astype(o_ref.dtype)


def _ones_column(v):
    pad = LANES - v.shape[-1] - 1
    return jnp.concatenate([v, jnp.ones(v.shape[:-1] + (1,), v.dtype), jnp.zeros(v.shape[:-1] + (pad,), v.dtype)], axis=-1)


def pflash(q, k, v, *, mode, tq, tk, gb=2, extras=(), window=NSA_WINDOW):
    B, G, hg, T, dk = q.shape
    dv = v.shape[-1]
    tq, tk = _pick_tile(T, tq), _pick_tile(T, tk)
    assert tk % LANES == 0 and tk % tq == 0
    gb = _pick_tile(G, gb)
    nq = T // tq
    last = lambda i: (i * tq + tq - 1) // tk
    if mode == "win":
        nk = max(last(i) - max(i * tq - window + 1, 0) // tk + 1 for i in range(nq))
        kidx = lambda i, j: jnp.maximum(last(i) - (nk - 1) + j, 0)
    elif mode == "sb":
        nk = T // tk
        kidx = lambda i, j: jnp.maximum(last(i) - j, 0)
    else:
        nk = T // tk
        kidx = lambda i, j: jnp.minimum(j, last(i))
    in_specs = [pl.BlockSpec((None, gb, hg, tq, dk), lambda b, g, i, j: (b, g, 0, i, 0)),
                pl.BlockSpec((None, gb, tk, dk), lambda b, g, i, j: (b, g, kidx(i, j), 0)),
                pl.BlockSpec((None, gb, tk, LANES), lambda b, g, i, j: (b, g, kidx(i, j), 0))]
    if mode == "fox":
        in_specs += [pl.BlockSpec((None, gb, hg, tq, LANES), lambda b, g, i, j: (b, g, 0, i, 0)),
                     pl.BlockSpec((None, gb, hg, 1, tk), lambda b, g, i, j: (b, g, 0, 0, kidx(i, j)))]
    if mode == "sel":
        nbp = extras[0].shape[-1]
        in_specs += [pl.BlockSpec((None, gb, tq, nbp), lambda b, g, i, j: (b, g, i, 0)),
                     pl.BlockSpec((nbp, tk), lambda b, g, i, j: (0, kidx(i, j)))]
    if mode == "sb":
        extras = (jnp.asarray(np.concatenate([np.tril(np.ones((tk, tk), np.float32), -1), np.ones((tk, LANES), np.float32)],
                                             axis=1), dtype=BF),)
        in_specs += [pl.BlockSpec((tk, tk + LANES), lambda b, g, i, j: (0, 0))]
    R = hg * tq
    return pl.pallas_call(
        functools.partial(_pflash_body, mode=mode, gb=gb, hg=hg, tq=tq, tk=tk, nk=nk, dv=dv, window=window),
        grid=(B, G // gb, nq, nk),
        in_specs=in_specs,
        out_specs=pl.BlockSpec((None, gb, hg, tq, dv), lambda b, g, i, j: (b, g, 0, i, 0)),
        out_shape=jax.ShapeDtypeStruct((B, G, hg, T, dv), BF),
        scratch_shapes=[pltpu.VMEM((gb, R, LANES), F32), pltpu.VMEM((gb, R, LANES), F32)],
        compiler_params=_cparams(("parallel", "parallel", "parallel", "arbitrary")),
        name="pflash_" + mode,
    )(q, k, _ones_column(v), *extras)


def _pad_heads(q):
    low = lax.broadcasted_iota(jnp.int32, (q.shape[0], LANES), 1) < HEAD_DIM
    tiles = []
    for h in range(8):
        t = q[:, (h // 2) * LANES:(h // 2 + 1) * LANES]
        if h % 2 != h // 4:
            t = pltpu.roll(t, HEAD_DIM, 1)
        tiles.append(jnp.where(low if h < 4 else jnp.logical_not(low), t, 0.0))
    return jnp.concatenate(tiles, axis=1)


def _gflash_body(*refs, mode, hg, tq, tk, nk, window):
    it = iter(refs)
    q_ref, k_ref, v_ref = next(it), next(it), next(it)
    cq_ref = ck_ref = sel_ref = e_ref = u_ref = None
    if mode == "fox":
        cq_ref, ck_ref = next(it), next(it)
    if mode == "sel":
        sel_ref, e_ref = next(it), next(it)
    if mode == "sb":
        u_ref = next(it)
    o_ref, m_s, l_s, acc_s = next(it), next(it), next(it), next(it)
    i, j = pl.program_id(1), pl.program_id(2)
    R = hg * tq
    last = (i * tq + tq - 1) // tk

    @pl.when(j == 0)
    def _():
        m_s[...] = jnp.full(m_s.shape, 0.0 if mode == "sb" else -jnp.inf, F32)
        l_s[...] = jnp.zeros_like(l_s)
        acc_s[...] = jnp.zeros_like(acc_s)

    if mode == "win":
        kt = last - (nk - 1) + j
        active = kt >= 0
        edge = active
    elif mode == "sb":
        kt = last - j
        active = kt >= 0
        edge = j == 0
    else:
        kt = j
        active = j <= last
        edge = j == last

    def tile(g, positional):
        q = jnp.concatenate([q_ref[:, (g * hg + h) * LANES:(g * hg + h + 1) * LANES] for h in range(hg)], axis=0)
        s3 = _dot_t(q, k_ref[...]).reshape(hg, tq, tk)
        mask = None
        if positional:
            qpos = i * tq + lax.broadcasted_iota(jnp.int32, (tq, tk), 0)
            kpos = kt * tk + lax.broadcasted_iota(jnp.int32, (tq, tk), 1)
            if mode == "sb":
                mask = kpos < qpos
            elif mode == "win":
                d = qpos - kpos
                mask = (d >= 0) & (d < window)
            else:
                mask = kpos <= qpos
        if mode == "sel":
            picked = _dot(sel_ref[g], e_ref[...]) > 0.5
            mask = picked if mask is None else mask & picked
        if mode == "fox":
            s3 = s3 - ck_ref[g]
        chunk = lambda a: [a[:, c * LANES:(c + 1) * LANES] for c in range(tk // LANES)]
        if mode == "sb":
            lk3 = -_softplus(s3)
            if mask is not None:
                lk3 = jnp.where(mask[None], lk3, 0.0)
            sufx = _dot(lk3.reshape(R, tk).astype(BF), u_ref[...])
            carry = m_s[g]
            e = jnp.concatenate([a + b + carry for a, b in zip(chunk((s3 + lk3).reshape(R, tk)), chunk(sufx))], axis=1)
            w3 = jnp.exp(e.reshape(hg, tq, tk))
            if mask is not None:
                w3 = jnp.where(mask[None], w3, 0.0)
            acc_s[g] += _dot(w3.reshape(R, tk).astype(BF), v_ref[...])
            m_s[g] = carry + sufx[:, tk:]
        else:
            if mask is not None:
                s3 = jnp.where(mask[None], s3, -jnp.inf)
            chunks = chunk(s3.reshape(R, tk))
            if mode == "fox":
                cq = cq_ref[g].reshape(R, LANES)
                chunks = [a + cq for a in chunks]
            m_prev = m_s[g]
            m_new = jnp.maximum(m_prev, jnp.max(functools.reduce(jnp.maximum, chunks), axis=1, keepdims=True))
            m_use = jnp.where(m_new > -jnp.inf, m_new, 0.0)
            p = jnp.concatenate([jnp.exp2(a - m_use) for a in chunks], axis=1).astype(BF)
            alpha = jnp.exp2(m_prev - m_use)
            l_s[g] = alpha * l_s[g] + _dot(p, jnp.ones((tk, LANES), BF))
            acc_s[g] = alpha * acc_s[g] + _dot(p, v_ref[...])
            m_s[g] = m_new

    @pl.when(active & edge)
    def _():
        for g in range(2):
            tile(g, True)

    if mode != "win":
        @pl.when(active & jnp.logical_not(edge))
        def _():
            for g in range(2):
                tile(g, False)

    @pl.when(j == nk - 1)
    def _():
        for g in range(2):
            acc = acc_s[g]
            if mode != "sb":
                l = l_s[g]
                acc = acc / jnp.where(l > 0.0, l, 1.0)
            heads = [acc[h * tq:(h + 1) * tq, g * HEAD_DIM:(g + 1) * HEAD_DIM] for h in range(hg)]
            o_ref[:, g * hg * HEAD_DIM:(g + 1) * hg * HEAD_DIM] = jnp.concatenate(heads, axis=1).astype(o_ref.dtype)


def gflash(qp, kv, *, mode, tq, tk, extras=(), window=NSA_WINDOW):
    B, T, _ = qp.shape
    hg = 4
    tq, tk = _pick_tile(T, tq), _pick_tile(T, tk)
    assert tk % LANES == 0 and tk % tq == 0
    nq = T // tq
    last = lambda i: (i * tq + tq - 1) // tk
    if mode == "win":
        nk = max(last(i) - max(i * tq - window + 1, 0) // tk + 1 for i in range(nq))
        kidx = lambda i, j: jnp.maximum(last(i) - (nk - 1) + j, 0)
    elif mode == "sb":
        nk = T // tk
        kidx = lambda i, j: jnp.maximum(last(i) - j, 0)
    else:
        nk = T // tk
        kidx = lambda i, j: jnp.minimum(j, last(i))
    in_specs = [pl.BlockSpec((None, tq, 8 * LANES), lambda b, i, j: (b, i, 0)),
                pl.BlockSpec((None, tk, LANES), lambda b, i, j: (b, kidx(i, j), 0)),
                pl.BlockSpec((None, tk, LANES), lambda b, i, j: (b, kidx(i, j), 1))]
    if mode == "fox":
        in_specs += [pl.BlockSpec((None, 2, hg, tq, LANES), lambda b, i, j: (b, 0, 0, i, 0)),
                     pl.BlockSpec((None, 2, hg, 1, tk), lambda b, i, j: (b, 0, 0, 0, kidx(i, j)))]
    if mode == "sel":
        nbp = extras[0].shape[-1]
        in_specs += [pl.BlockSpec((None, 2, tq, nbp), lambda b, i, j: (b, 0, i, 0)),
                     pl.BlockSpec((nbp, tk), lambda b, i, j: (0, kidx(i, j)))]
    if mode == "sb":
        extras = (jnp.asarray(np.concatenate([np.tril(np.ones((tk, tk), np.float32), -1), np.ones((tk, LANES), np.float32)],
                                             axis=1), dtype=BF),)
        in_specs += [pl.BlockSpec((tk, tk + LANES), lambda b, i, j: (0, 0))]
    R = hg * tq
    return pl.pallas_call(
        functools.partial(_gflash_body, mode=mode, hg=hg, tq=tq, tk=tk, nk=nk, window=window),
        grid=(B, nq, nk),
        in_specs=in_specs,
        out_specs=pl.BlockSpec((None, tq, 2 * hg * HEAD_DIM), lambda b, i, j: (b, i, 0)),
        out_shape=jax.ShapeDtypeStruct((B, T, 2 * hg * HEAD_DIM), BF),
        scratch_shapes=[pltpu.VMEM((2, R, LANES), F32), pltpu.VMEM((2, R, LANES), F32), pltpu.VMEM((2, R, LANES), F32)],
        compiler_params=_cparams(("parallel", "parallel", "arbitrary")),
        name="gflash_" + mode,
    )(qp, kv, kv, *extras)


def _cmpsel_body(q_ref, kc_ref, vc_ref, oc_ref, sel_ref, *, hg, tq, nbc, nbs, n_sel, pos0):
    for bi in range(q_ref.shape[0]):
        for g in range(q_ref.shape[1]):
            _cmpsel_chain(q_ref.at[bi, g], kc_ref.at[bi, g], vc_ref.at[bi, g], oc_ref.at[bi, g], sel_ref.at[bi, g],
                          hg=hg, tq=tq, nbc=nbc, nbs=nbs, n_sel=n_sel, pos0=pos0)


def _cmpsel_chain(q_ref, kc_ref, vc_ref, oc_ref, sel_ref, *, hg, tq, nbc, nbs, n_sel, pos0):
    qi = pl.program_id(1)
    R = hg * tq
    nbp = kc_ref.shape[0]
    s = _dot_t(q_ref[...], kc_ref[...], precision=HI)
    n_r = lax.broadcasted_iota(jnp.int32, (R, nbp), 1)
    r_r = lax.broadcasted_iota(jnp.int32, (R, nbp), 0)
    qpos_r = pos0 + qi * tq + (r_r % tq)
    cmask = (n_r * NSA_BLOCK + (NSA_BLOCK - 1) <= qpos_r) & (n_r < nbc)
    s = jnp.where(cmask, s, -jnp.inf)
    m = jnp.max(s, axis=1, keepdims=True)
    m = jnp.where(m > -jnp.inf, m, 0.0)
    ex = jnp.exp(s - m)
    den = jnp.sum(ex, axis=1, keepdims=True)
    p = ex / jnp.where(den > 0.0, den, 1.0)
    oc_ref[...] = _dot(p.astype(BF), vc_ref[...].astype(BF)).astype(oc_ref.dtype)
    tt = lax.broadcasted_iota(jnp.int32, (tq, R), 0)
    rr = lax.broadcasted_iota(jnp.int32, (tq, R), 1)
    fold = (rr % tq == tt).astype(BF)
    a, b, c = _split3(p)
    imp = _dot(fold, a) + _dot(fold, b) + _dot(fold, c)
    n = lax.broadcasted_iota(jnp.int32, (tq, nbp), 1)
    qpos = pos0 + qi * tq + lax.broadcasted_iota(jnp.int32, (tq, nbp), 0)
    cur = qpos // NSA_BLOCK
    forced = (n == 0) | (n == cur) | (n == cur - 1)
    score = jnp.where(forced, float(hg + 1), jnp.where(n <= cur, imp, -1.0))
    score = jnp.where(n < nbs, score, -2.0)
    nf = n.astype(F32)
    sel = jnp.zeros((tq, nbp), F32)
    for _ in range(n_sel):
        mx = jnp.max(score, axis=1, keepdims=True)
        idx = jnp.min(jnp.where(score == mx, nf, 1e9), axis=1, keepdims=True)
        hit = nf == idx
        sel = jnp.where(hit & (mx >= 0.0), 1.0, sel)
        score = jnp.where(hit, -3.0, score)
    sel_ref[...] = sel.astype(sel_ref.dtype)


def nsa_cmp_sel(qf, kc, vc, *, tq, nbc, nbs, pos0, bb):
    B, G, nqt, R, d = qf.shape
    hg = R // tq
    nbp = kc.shape[2]
    n_sel = min(NSA_TOPN, nbs)
    bb = _pick_tile(B, bb)
    return pl.pallas_call(
        functools.partial(_cmpsel_body, hg=hg, tq=tq, nbc=nbc, nbs=nbs, n_sel=n_sel, pos0=pos0),
        grid=(B // bb, nqt),
        in_specs=[pl.BlockSpec((bb, G, None, R, d), lambda b, i: (b, 0, i, 0, 0)),
                  pl.BlockSpec((bb, G, nbp, d), lambda b, i: (b, 0, 0, 0)),
                  pl.BlockSpec((bb, G, nbp, d), lambda b, i: (b, 0, 0, 0))],
        out_specs=[pl.BlockSpec((bb, G, None, R, d), lambda b, i: (b, 0, i, 0, 0)),
                   pl.BlockSpec((bb, G, None, tq, nbp), lambda b, i: (b, 0, i, 0, 0))],
        out_shape=[jax.ShapeDtypeStruct((B, G, nqt, R, d), BF), jax.ShapeDtypeStruct((B, G, nqt, tq, nbp), BF)],
        compiler_params=_cparams(("parallel", "parallel")),
        name="nsa_cmp_sel",
    )(qf, kc, vc)


PAGE = 128
DEC_PAGES_PER_STEP = 16
DEC_SEQS_PER_STEP = 4
NEW_PAD = PAGE


def _dec_softmax_step(s, vs, m_s, l_s, acc_s):
    m_prev = m_s[:, :1]
    m_new = jnp.maximum(m_prev, jnp.max(s, axis=1, keepdims=True))
    m_use = jnp.where(m_new > -jnp.inf, m_new, 0.0)
    p = jnp.exp(s - m_use)
    alpha = jnp.exp(m_prev - m_use)
    l_s[...] = jnp.broadcast_to(alpha * l_s[:, :1] + jnp.sum(p, axis=1, keepdims=True), l_s.shape)
    w = vs[0].shape[1]
    pv = _dot_t(p[:, :w].astype(BF), vs[0])
    for n in range(1, len(vs)):
        pv = pv + _dot_t(p[:, n * w:(n + 1) * w].astype(BF), vs[n])
    acc_s[...] = alpha * acc_s[...] + pv
    m_s[...] = jnp.broadcast_to(m_new, m_s.shape)


def _dec_body(pt_ref, *refs, mode, G, bb, nc, n_new, rows_per_t, t_period, wlen, window):
    del pt_ref
    it = iter(refs)
    q_ref = next(it)
    cq_ref = ck_ref = cknew_ref = selx_ref = e2_ref = selnew_ref = u_ref = None
    if mode == "fox":
        cq_ref, ck_ref, cknew_ref = next(it), next(it), next(it)
    if mode == "sel":
        selx_ref, e2_ref, selnew_ref = next(it), next(it), next(it)
    if mode == "sb":
        u_ref = next(it)
    pages = [[next(it) for _ in range(G)] for _ in range(bb)]
    new_ref = next(it)
    o_ref, m_s, l_s, acc_s = next(it), next(it), next(it), next(it)
    c = pl.program_id(1)
    R = q_ref.shape[1]

    def row_t(shape):
        r = lax.broadcasted_iota(jnp.int32, shape, 0)
        return (r % t_period) // rows_per_t

    def split_kv(x):
        if mode == "mla":
            kb = x.astype(BF)
            return kb, kb[:256]
        return x[:128].astype(BF), x[128:].astype(BF)

    def new_rows(bi):
        q = q_ref[bi]
        kb, vb = split_kv(new_ref[bi])
        s = _dot(q, kb)
        t = row_t(s.shape)
        key = lax.broadcasted_iota(jnp.int32, s.shape, 1)
        if mode == "sb":
            mask = (key < t) & (key < n_new)
            lk = jnp.where(mask, -_softplus(s), 0.0)
            hi = lk.astype(BF)
            lo = (lk - hi.astype(F32)).astype(BF)
            suf = _dot(hi, u_ref[...]) + _dot(lo, u_ref[...])
            w = jnp.where(mask, jnp.exp(s + lk + suf), 0.0)
            acc_s[bi] += _dot_t(w.astype(BF), vb)
            m_s[bi] = jnp.broadcast_to(jnp.sum(lk, axis=1, keepdims=True), (R, LANES))
        else:
            mask = (key <= t) & (key < n_new)
            if mode == "sel":
                mask = mask & (selnew_ref[bi] > 0.5)
            if mode == "fox":
                s = s + cq_ref[bi][:, :1] - cknew_ref[bi]
            _dec_softmax_step(jnp.where(mask, s, -jnp.inf), [vb], m_s.at[bi], l_s.at[bi], acc_s.at[bi])

    def past_pages(bi):
        q = q_ref[bi]
        kvs = [split_kv(p[...]) for p in pages[bi]]
        zs = [_dot(q, kb) for kb, _ in kvs]
        if mode == "sb":
            u = u_ref[...]
            carry = m_s[bi][:, :1]
            pv = None
            for z, (_, vb) in zip(zs, kvs):
                lk = -_softplus(z)
                hi = lk.astype(BF)
                lo = (lk - hi.astype(F32)).astype(BF)
                suf = _dot(hi, u) + _dot(lo, u)
                w = jnp.exp(z + lk + suf + carry)
                d = _dot_t(w.astype(BF), vb)
                pv = d if pv is None else pv + d
                carry = carry + jnp.sum(lk, axis=1, keepdims=True)
            acc_s[bi] += pv
            m_s[bi] = jnp.broadcast_to(carry, (R, LANES))
        else:
            s = jnp.concatenate(zs, axis=1) if G > 1 else zs[0]
            if mode == "fox":
                s = s + cq_ref[bi][:, :1] - ck_ref[bi]
            if mode == "sel":
                s = jnp.where(_dot(selx_ref[bi], e2_ref[...]) > 0.5, s, -jnp.inf)
            if mode == "win":
                widx = c * (G * PAGE) + lax.broadcasted_iota(jnp.int32, s.shape, 1)
                s = jnp.where(widx > row_t(s.shape) + (wlen - window), s, -jnp.inf)
            _dec_softmax_step(s, [vb for _, vb in kvs], m_s.at[bi], l_s.at[bi], acc_s.at[bi])

    @pl.when(c == 0)
    def _():
        m_s[...] = jnp.full(m_s.shape, 0.0 if mode == "sb" else -jnp.inf, F32)
        l_s[...] = jnp.zeros_like(l_s)
        acc_s[...] = jnp.zeros_like(acc_s)
        for bi in range(bb):
            new_rows(bi)

    for bi in range(bb):
        past_pages(bi)

    @pl.when(c == nc - 1)
    def _():
        if mode == "sb":
            o_ref[...] = acc_s[...]
        else:
            l = l_s[...][:, :, :1]
            o_ref[...] = acc_s[...] / jnp.where(l > 0.0, l, 1.0)


def dec_attn(pt, q, cache, new, *, mode, extras=(), n_new, rows_per_t, t_period, window=NSA_WINDOW):
    B, n_pages = pt.shape
    _, R, Dq = q.shape
    W = cache.shape[1]
    G = _pick_tile(n_pages, DEC_PAGES_PER_STEP)
    bb = _pick_tile(B, DEC_SEQS_PER_STEP)
    nc = n_pages // G
    dv = 256 if mode == "mla" else 128
    if mode == "sb":
        page_idx = lambda b, c, pt, bi, p: (pt[b * bb + bi, n_pages - 1 - (c * G + p)], 0, 0)
    elif mode == "win":
        page_idx = lambda b, c, pt, bi, p: (b * bb + bi, 0, c * G + p)
    else:
        page_idx = lambda b, c, pt, bi, p: (pt[b * bb + bi, c * G + p], 0, 0)
    per_b = lambda shape: pl.BlockSpec((bb,) + shape, lambda b, c, pt: (b,) + (0,) * len(shape))
    in_specs = [per_b((R, Dq))]
    if mode == "fox":
        in_specs += [per_b((R, LANES)), pl.BlockSpec((bb, R, G * PAGE), lambda b, c, pt: (b, 0, c)), per_b((R, NEW_PAD))]
    if mode == "sel":
        in_specs += [pl.BlockSpec((bb, None, R, 2 * G), lambda b, c, pt: (b, c, 0, 0)),
                     pl.BlockSpec((2 * G, G * PAGE), lambda b, c, pt: (0, 0)), per_b((R, NEW_PAD))]
    if mode == "sb":
        in_specs += [pl.BlockSpec((PAGE, PAGE), lambda b, c, pt: (0, 0))]
    in_specs += [pl.BlockSpec((None, W, PAGE), functools.partial(page_idx, bi=bi, p=p)) for bi in range(bb) for p in range(G)]
    in_specs += [per_b((W, NEW_PAD))]
    return pl.pallas_call(
        functools.partial(_dec_body, mode=mode, G=G, bb=bb, nc=nc, n_new=n_new, rows_per_t=rows_per_t, t_period=t_period,
                          wlen=n_pages * PAGE, window=window),
        grid_spec=pltpu.PrefetchScalarGridSpec(
            num_scalar_prefetch=1,
            grid=(B // bb, nc),
            in_specs=in_specs,
            out_specs=per_b((R, dv)),
            scratch_shapes=[pltpu.VMEM((bb, R, LANES), F32), pltpu.VMEM((bb, R, LANES), F32), pltpu.VMEM((bb, R, dv), F32)],
        ),
        out_shape=jax.ShapeDtypeStruct((B, R, dv), F32),
        compiler_params=_cparams(("parallel", "arbitrary")),
        name="dec_" + mode,
    )(pt, q, *extras, *([cache] * (bb * G)), new)


def _page_rows_body(pt_ref, *refs, G):
    del pt_ref
    o_ref = refs[G]
    o_ref[...] = jnp.concatenate([r[...] for r in refs[:G]], axis=0)


def gather_page_rows(pt, cache_rows):
    B, n_pages = pt.shape
    _, H, W = cache_rows.shape
    G = _pick_tile(n_pages, 32)
    return pl.pallas_call(
        functools.partial(_page_rows_body, G=G),
        grid_spec=pltpu.PrefetchScalarGridSpec(
            num_scalar_prefetch=1,
            grid=(B, n_pages // G),
            in_specs=[pl.BlockSpec((None, H, W), lambda b, c, pt, p=p: (pt[b, c * G + p], 0, 0)) for p in range(G)],
            out_specs=pl.BlockSpec((None, G * H, W), lambda b, c, pt: (b, c, 0)),
        ),
        out_shape=jax.ShapeDtypeStruct((B, n_pages * H, W), cache_rows.dtype),
        compiler_params=_cparams(("parallel", "parallel")),
        name="gather_page_rows",
    )(pt, *([cache_rows] * G))


def _cmp_pages_body(pt_ref, *refs, G, nc, n_pages):
    del pt_ref
    pages = refs[:G]
    w_ref, b_ref, o_ref, x_s = refs[G:]
    c = pl.program_id(1)
    F = pages[0].shape[0]
    for p in range(G):
        x_s[pl.ds(pl.multiple_of((c * G + p) * F, F), F), :] = pages[p][...]

    @pl.when(c == nc - 1)
    def _():
        for kv in range(2):
            for g in range(NSA_GROUPS):
                f0 = (kv * NSA_GROUPS + g) * HEAD_DIM
                x = jnp.concatenate([x_s[pl.ds(f0 + d, n_pages, stride=F), :] for d in range(HEAD_DIM)], axis=1)
                col = (kv * NSA_GROUPS + g) * LANES
                o_ref[:, col:col + LANES] = _dot(x.astype(BF), w_ref[kv]) + b_ref[:, col:col + LANES]


def compress_pages(pt, cache_t, w2, bias):
    B, n_pages = pt.shape
    F = cache_t.shape[1]
    G = _pick_tile(n_pages, 16)
    nc = n_pages // G
    return pl.pallas_call(
        functools.partial(_cmp_pages_body, G=G, nc=nc, n_pages=n_pages),
        grid_spec=pltpu.PrefetchScalarGridSpec(
            num_scalar_prefetch=1,
            grid=(B, nc),
            in_specs=[pl.BlockSpec((None, F, PAGE), lambda b, c, pt, p=p: (pt[b, c * G + p], 0, 0)) for p in range(G)]
            + [pl.BlockSpec(w2.shape, lambda b, c, pt: (0, 0, 0)), pl.BlockSpec(bias.shape, lambda b, c, pt: (0, 0))],
            out_specs=pl.BlockSpec((None, n_pages, 4 * LANES), lambda b, c, pt: (b, 0, 0)),
            scratch_shapes=[pltpu.VMEM((n_pages * F, PAGE), F32)],
        ),
        out_shape=jax.ShapeDtypeStruct((B, n_pages, 4 * LANES), F32),
        compiler_params=_cparams(("parallel", "arbitrary")),
        name="compress_pages",
    )(pt, *([cache_t] * G), w2, bias)


def _page_prefix_body(within_ref, tot_ref, l_ref, o_ref):
    a, b, c = _split3(tot_ref[...])
    l = l_ref[...]
    o_ref[...] = within_ref[...] + _dot(l, a) + _dot(l, b) + _dot(l, c)


def page_prefix(y, lstrict):
    B, npg, W2 = y.shape
    W = W2 // 2
    return pl.pallas_call(
        _page_prefix_body,
        grid=(B,),
        in_specs=[pl.BlockSpec((None, npg, W), lambda b: (b, 0, 0)), pl.BlockSpec((None, npg, W), lambda b: (b, 0, 1)),
                  pl.BlockSpec((npg, npg), lambda b: (0, 0))],
        out_specs=pl.BlockSpec((None, npg, W), lambda b: (b, 0, 0)),
        out_shape=jax.ShapeDtypeStruct((B, npg, W), F32),
        compiler_params=_cparams(("parallel",)),
        name="page_prefix",
    )(y, y, lstrict)


def _swap_halves(w, hd):
    K, N = w.shape
    return w.reshape(K, N // hd, 2, hd // 2)[:, :, ::-1, :].reshape(K, N)


def _pad_last(a, n):
    return jnp.pad(a, [(0, 0)] * (a.ndim - 1) + [(0, n - a.shape[-1])])


def _rope_tabs(pos, d):
    inv = ROPE_THETA ** (-jnp.arange(0, d, 2, dtype=F32) / d)
    ang = pos.astype(F32)[:, None] * inv[None, :]
    c, s = jnp.cos(ang), jnp.sin(ang)
    return jnp.concatenate([c, c], axis=1), jnp.concatenate([-s, s], axis=1)


def _prep_weights(W):
    D = W["ev_w_in"].shape[1]
    P = {}
    w = W["ev_w_in"][0]
    kr = w[:, 1280:1312]
    P["ev_wext"] = jnp.concatenate([w[:, :1280], _pad_last(kr, 128), _pad_last(_swap_halves(kr, 32), 128)], axis=1).astype(BF)
    wuq = W["ev_w_uq"][0]
    wn, wr = wuq[:, :, :MLA_NOPE].reshape(256, 512), wuq[:, :, MLA_NOPE:].reshape(256, 256)
    P["ev_wuq"] = jnp.concatenate([wn, wr, _swap_halves(wr, MLA_ROPE)], axis=1).astype(BF)
    wuk, wuv = W["ev_w_uk"][0], W["ev_w_uv"][0]
    eye8 = jnp.eye(MLA_HEADS, dtype=F32)
    P["ev_wuk"] = wuk.reshape(256, 512).astype(BF)
    P["ev_wuv"] = wuv.reshape(256, 512).astype(BF)
    P["ev_wabs"] = jnp.einsum("chn,hH->hnHc", wuk, eye8).reshape(512, 2048).astype(BF)
    P["ev_wuv_bd"] = jnp.einsum("chv,hH->hcHv", wuv, eye8).reshape(2048, 512).astype(BF)
    P["ev_wout"] = W["ev_w_out"][0].astype(BF)
    P["ev_gq"] = W["ev_q_norm"][0][None, :]
    P["ev_gkv"] = W["ev_kv_norm"][0][None, :]

    w = W["od_w_in"][0]
    seg = lambda a, b: w[:, a:b]
    nq, ck, cv, sk, sv, wk, wv = seg(0, 512), seg(512, 640), seg(640, 768), seg(768, 896), seg(896, 1024), seg(1024, 1152), seg(1152, 1280)
    gl, fq, fk, fv, fl = seg(1280, 1304), seg(1304, 1816), seg(1816, 1944), seg(1944, 2072), seg(2072, 2080)
    sw = lambda a: _swap_halves(a, HEAD_DIM)
    P["od_wext"] = jnp.concatenate([nq, sw(nq), ck, sw(ck), cv, sk, sw(sk), sv, wk, sw(wk), wv, _pad_last(gl, 128), fq, fk, fv,
                                    _pad_last(fl, 128)], axis=1).astype(BF)
    P["od_fb"] = _pad_last(W["od_fox_fb"][0][None, :], 128)
    P["od_wout"] = W["od_w_out"][0].astype(BF)
    eye2 = jnp.eye(2, dtype=F32)
    wbig = jnp.einsum("klde,kK,gG->lkgdKGe", W["od_cmp_w"][0], eye2, eye2).reshape(NSA_BLOCK * 256, 256)
    pe_flat = jnp.broadcast_to(W["od_cmp_pe"][0].transpose(1, 0, 2)[:, :, None, :], (NSA_BLOCK, 2, 2, HEAD_DIM)).reshape(1, -1)
    P["cmp_w"] = wbig.astype(BF)
    P["cmp_b"] = linear(jnp.pad(pe_flat, ((0, 7), (0, 0))), wbig, mode="hi", tn=128, name="cmp_bias")[:1]
    P["cmp_w2"] = jnp.einsum("klde,hH->kdhlHe", W["od_cmp_w"][0], eye2).reshape(2, HEAD_DIM * PAGE, 2 * HEAD_DIM).astype(BF)
    P["cmp_b2"] = jnp.broadcast_to(P["cmp_b"].reshape(2, 2, 1, HEAD_DIM), (2, 2, 2, HEAD_DIM)).reshape(1, 8 * HEAD_DIM)
    r = np.arange(128)
    cidx = np.arange(1536)
    P["gate_expand"] = jnp.asarray(((r[:, None] // 3 == (cidx[None, :] % 512) // 64) & (r[:, None] % 3 == cidx[None, :] // 512)
                                    & (r[:, None] < 24)).astype(np.float32), dtype=BF)
    i = np.arange(PAGE)
    P["scan_w"] = jnp.asarray(np.concatenate([i[:, None] <= i[None, :], np.ones((PAGE, PAGE), bool)], axis=1).astype(np.float32),
                              dtype=BF)
    P["moe_wg"] = W["moe_w_gate"].astype(BF)
    P["moe_wu"] = W["moe_w_up"].astype(BF)
    P["moe_wd"] = W["moe_w_down"].astype(BF)
    return P


def _mods(c, W, l, T, per_token):
    B, D = c.shape
    m = linear(c, W["ada_w"], W["ada_b"].reshape(W["ada_b"].shape[0], 1, -1), widx=l, act="silu", name="ada_mod").reshape(B, 6, D)
    if per_token:
        return [jnp.repeat(m[:, j], T, axis=0)[None] for j in range(6)]
    return [m[:, j][:, None, :] for j in range(6)]


def _heads_major(a, B, T, G, hg):
    return a.reshape(B, T, G, hg, HEAD_DIM).transpose(0, 2, 3, 1, 4)


def _kv_major(a, B, T, G):
    return a.reshape(B, T, G, HEAD_DIM).transpose(0, 2, 1, 3)


def _from_heads_major(o, N):
    B, G, hg, T, d = o.shape
    return o.transpose(0, 3, 1, 2, 4).reshape(N, G * hg * d)


def _block_diag_q(q, B, T):
    q5 = q.reshape(B, T, 2, 4, HEAD_DIM).transpose(0, 2, 1, 3, 4).reshape(B, 2, T * 4, 1, HEAD_DIM)
    eye = jnp.eye(2, dtype=q.dtype)[None, :, None, :, None]
    return (q5 * eye).reshape(B, 2 * T * 4, 2 * HEAD_DIM)


def _pick_diag(o, B, T):
    o6 = o.reshape(B, 2, T, 4, 2, HEAD_DIM)
    return jnp.stack([o6[:, 0, :, :, 0], o6[:, 1, :, :, 1]], axis=2).reshape(B * T, 512)


def _pad_new(rows, B, T):
    return jnp.pad(rows.reshape(B, T, -1), ((0, 0), (0, NEW_PAD - T), (0, 0))).transpose(0, 2, 1)


def _token_minor(cache):
    n, t = cache.shape[:2]
    return jnp.moveaxis(cache, 1, -1).reshape(n, -1, t)


def _head_rows(T):
    kv, t, h = np.meshgrid(np.arange(2), np.arange(T), np.arange(4), indexing="ij")
    return (kv * 4 + h).reshape(-1), t.reshape(-1)


def _scan_logf(x_rows, P):
    B, rows, _ = x_rows.shape
    npg = rows // 8
    y = linear(x_rows.reshape(B * rows, PAGE), P["scan_w"], mode="split3", tm=2048, tn=2 * PAGE, name="scan_within")
    r = np.arange(rows)
    earlier = jnp.asarray(((r[:, None] // 8 > r[None, :] // 8) & (r[:, None] % 8 == r[None, :] % 8)).astype(np.float32), dtype=BF)
    cs = page_prefix(y.reshape(B, rows, 2 * PAGE), earlier)
    return cs.reshape(B, npg, 8, PAGE).transpose(0, 2, 1, 3).reshape(B, 8, npg * PAGE)


def _trunk(x, c, W, P, past):
    B, T, D = x.shape
    N = B * T
    sample = past is not None
    x2 = x.reshape(N, D)
    tm = N if sample else _pick_tile(T, 256)
    mod_rows = N if sample else T
    kw = dict(tm=tm, mod_rows=mod_rows)
    kw_moe = dict(tm=N if sample else _pick_tile(T, MOE_TOKEN_TILE), mod_rows=mod_rows)
    pos0 = past["len"] if sample else 0
    pos = pos0 + jnp.arange(T, dtype=jnp.int32)
    rep = (lambda a: jnp.tile(a, (B, 1))) if sample else (lambda a: a)
    tab_rows = N if sample else T
    out = {}

    sh_a, sc_a, gt_a, sh_m, sc_m, gt_m = _mods(c, W, 0, T, sample)
    c32, s32 = _rope_tabs(pos, MLA_ROPE)
    tabs = [rep(_pad_last(c32, 128)), rep(_pad_last(s32, 128)), rep(jnp.tile(c32, (1, 8))), rep(jnp.tile(s32, (1, 8)))]
    g0 = W["norm_mix"][0][None, :]
    if not sample:
        sbq, sbrows, mla, qm, km, vm, sbkv = even_front(x2, g0, sc_a, sh_a, P["ev_wext"], tabs, P["ev_gq"], P["ev_gkv"],
                                                         P["ev_wuq"], P["ev_wuk"], P["ev_wuv"], absorbed=False,
                                                         tab_rows=tab_rows, **kw)
        o_sb = gflash(sbq.reshape(B, T, -1), sbkv.reshape(B, T, -1), mode="sb", **ATT_TILES).reshape(N, 512)
        o_mla = pflash(qm[:, :, None], km, vm, mode="causal", **MLA_TILES)
        mix = jnp.concatenate([o_sb, _from_heads_major(o_mla, N)], axis=1)
    else:
        pt = past["pt"]
        sbq, sbrows, mla, qlat, qr = even_front(x2, g0, sc_a, sh_a, P["ev_wext"], tabs, P["ev_gq"], P["ev_gkv"], P["ev_wuq"],
                                                P["ev_wabs"], None, absorbed=True, tab_rows=tab_rows, **kw)
        u = jnp.asarray(np.tril(np.ones((PAGE, PAGE), np.float32), -1), dtype=BF)
        n_pool = past["sb"].shape[0]
        o = dec_attn(pt, _block_diag_q(sbq, B, T), _token_minor(past["sb"]), _pad_new(sbrows, B, T), mode="sb",
                     extras=(u,), n_new=T, rows_per_t=4, t_period=4 * T)
        o_sb = _pick_diag(o, B, T).astype(BF)
        qcat = jnp.concatenate([qlat.reshape(B, T, 8, 256), qr.reshape(B, T, 8, 32)], axis=-1).reshape(B, T * 8, 288)
        o_lat = dec_attn(pt, qcat, _token_minor(past["mla"]), _pad_new(mla, B, T), mode="mla", n_new=T,
                         rows_per_t=8, t_period=8 * T)
        o_mla = linear(o_lat.reshape(N, 2048), P["ev_wuv_bd"], out_dtype=BF, name="mla_v_up")
        mix = jnp.concatenate([o_sb, o_mla], axis=1)
    out["sb"] = sbrows.reshape(1, B, T, 2, SB_KV_HEADS, HEAD_DIM)
    out["mla"] = mla.reshape(1, B, T, 288)
    x2 = out_proj_res(mix, P["ev_wout"], x2, gt_a, **kw)
    x2 = moe_layer(x2, W["norm_ffn"][0][None, :], sc_m, sh_m, gt_m, P["wr"][0], P["br"][0],
                   P["moe_wg"], P["moe_wu"], P["moe_wd"], 0, **kw_moe)

    sh_a, sc_a, gt_a, sh_m, sc_m, gt_m = _mods(c, W, 1, T, sample)
    c64, s64 = _rope_tabs(pos, HEAD_DIM)
    tabs = [rep(jnp.tile(c64, (1, 8))), rep(jnp.tile(s64, (1, 8))), rep(jnp.tile(c64, (1, 2))), rep(jnp.tile(s64, (1, 2)))]
    nq, nqf, cmp_rows, slc_rows, win_rows, gates, fq, fox_rows, logf, *kvb = odd_front(
        x2, W["norm_mix"][1][None, :], sc_a, sh_a, P["od_wext"], tabs, P["od_fb"], tab_rows=tab_rows, packed=not sample, **kw)
    logf8 = logf[:, :8]
    G, hg = NSA_GROUPS, NSA_HEADS // NSA_GROUPS
    if not sample:
        nb = T // NSA_BLOCK
        nbp = _round_up(nb, LANES)
        kcv = linear(cmp_rows.astype(BF).reshape(B * nb, NSA_BLOCK * 256), P["cmp_w"], P["cmp_b"], tm=128, tn=256, name="compress")
        kcv = kcv.reshape(B, nb, 2, G, HEAD_DIM).transpose(2, 0, 3, 1, 4)
        kcv = jnp.pad(kcv, ((0, 0), (0, 0), (0, 0), (0, nbp - nb), (0, 0)))
        tq = _pick_tile(T, 256)
        nqt = T // tq
        qf = nqf.reshape(B, nqt, tq, G, hg, HEAD_DIM).transpose(0, 3, 1, 4, 2, 5).reshape(B, G, nqt, hg * tq, HEAD_DIM)
        oc, sel = nsa_cmp_sel(qf, kcv[0], kcv[1], tq=tq, nbc=nb, nbs=nb, pos0=0, bb=2)
        oc = oc.reshape(B, G, nqt, hg, tq, HEAD_DIM).transpose(0, 2, 4, 1, 3, 5).reshape(N, 512)
        sel = sel.reshape(B, G, T, nbp)
        expand = jnp.asarray((np.arange(nbp)[:, None] == np.arange(T)[None, :] // NSA_BLOCK).astype(np.float32), dtype=BF)
        slb, wnb, fxb = (a.reshape(B, T, -1) for a in kvb)
        nq3 = nq.reshape(B, T, -1)
        osel = gflash(nq3, slb, mode="sel", extras=(sel, expand), **ATT_TILES).reshape(N, 512)
        ow = gflash(nq3, wnb, mode="win", **ATT_TILES).reshape(N, 512)
        cs = _scan_logf(logf8.reshape(B, T // PAGE, PAGE, 8).transpose(0, 1, 3, 2).reshape(B, T // PAGE * 8, PAGE), P)
        ch = (cs * LOG2E).reshape(B, 2, 4, T)
        cq = jnp.broadcast_to(ch[..., None], (B, 2, 4, T, LANES))
        of = gflash(fq.reshape(B, T, -1), fxb, mode="fox", extras=(cq, ch[:, :, :, None, :]), **FOX_TILES).reshape(N, 512)
        wb = min(NSA_WINDOW, T)
        out["win"] = win_rows.reshape(1, B, T, 2, G, HEAD_DIM)[:, :, T - wb:]
    else:
        n_pages = pt.shape[1]
        L = n_pages * PAGE
        nbc = L // NSA_BLOCK
        nbs = nbc + 1
        nbp = _round_up(nbs, LANES)
        kcv = compress_pages(pt, _token_minor(past["cmp"]), P["cmp_w2"], P["cmp_b2"])
        kcv = kcv.reshape(B, n_pages, 2, G, 2, HEAD_DIM).transpose(2, 0, 3, 1, 4, 5).reshape(2, B, G, nbc, HEAD_DIM)
        kcv = jnp.pad(kcv, ((0, 0), (0, 0), (0, 0), (0, nbp - nbc), (0, 0)))
        qf = nqf.reshape(B, T, G, hg, HEAD_DIM).transpose(0, 2, 3, 1, 4).reshape(B, G, 1, hg * T, HEAD_DIM)
        oc, sel = nsa_cmp_sel(qf, kcv[0], kcv[1], tq=T, nbc=nbc, nbs=nbs, pos0=pos0, bb=4)
        oc = oc.reshape(B, G, hg, T, HEAD_DIM).transpose(0, 3, 1, 2, 4).reshape(N, 512)
        sel = sel.reshape(B, G, T, nbp)
        Gp = _pick_tile(n_pages, DEC_PAGES_PER_STEP)
        nc = n_pages // Gp
        selr = jnp.broadcast_to(sel[:, :, :, None, :], (B, G, T, hg, nbp)).reshape(B, G * T * hg, nbp)
        selx = selr[:, :, :2 * n_pages].reshape(B, G * T * hg, nc, 2 * Gp).transpose(0, 2, 1, 3)
        selnew = jnp.broadcast_to(selr[:, :, 2 * n_pages:2 * n_pages + 1].astype(F32), (B, G * T * hg, NEW_PAD))
        e2 = jnp.asarray((np.arange(2 * Gp)[:, None] == np.arange(Gp * PAGE)[None, :] // NSA_BLOCK).astype(np.float32), dtype=BF)
        rk = dict(n_new=T, rows_per_t=4, t_period=4 * T)
        qbd = _block_diag_q(nq, B, T)
        osel = dec_attn(pt, qbd, _token_minor(past["slc"]), _pad_new(slc_rows, B, T), mode="sel",
                        extras=(selx, e2, selnew), **rk)
        win_buf = past["win"]
        npw = win_buf.shape[1] // PAGE
        pt_win = jnp.zeros((B, npw), jnp.int32)
        ow = dec_attn(pt_win, qbd, _token_minor(win_buf), _pad_new(win_rows, B, T), mode="win", **rk)
        cpast = _scan_logf(gather_page_rows(pt, _token_minor(past["logf"])), P)
        lnew = logf8.reshape(B, T, 8)
        run = cpast[:, :, -1]
        cnew = []
        for tt in range(T):
            run = run + lnew[:, tt, :]
            cnew.append(run)
        cnew = jnp.stack(cnew, axis=1)
        hrow, trow = _head_rows(T)
        ck32 = cpast[:, hrow, :]
        cq32 = jnp.broadcast_to(cnew[:, trow, hrow][:, :, None], (B, len(hrow), LANES))
        cknew = _pad_last(cnew.transpose(0, 2, 1)[:, hrow, :], NEW_PAD)
        of = dec_attn(pt, _block_diag_q(fq, B, T), _token_minor(past["fox"]), _pad_new(fox_rows, B, T), mode="fox",
                      extras=(cq32, ck32, cknew), **rk)
        osel, ow, of = (_pick_diag(a, B, T).astype(BF) for a in (osel, ow, of))
        out["win"] = jnp.concatenate([win_buf, win_rows.reshape(B, T, 2, G, HEAD_DIM)], axis=1)[None, :, T:]
    mix = nsa_combine(oc, osel, ow, gates, P["gate_expand"], of, tm=tm)
    out["cmp"] = cmp_rows.reshape(1, B, T, 2, G, HEAD_DIM)
    out["slc"] = slc_rows.reshape(1, B, T, 2, G, HEAD_DIM)
    out["fox"] = fox_rows.reshape(1, B, T, 2, FOX_KV_HEADS, HEAD_DIM)
    out["logf"] = logf8.reshape(1, B, T, FOX_HEADS)
    x2 = out_proj_res(mix, P["od_wout"], x2, gt_a, **kw)
    x2 = moe_layer(x2, W["norm_ffn"][1][None, :], sc_m, sh_m, gt_m, P["wr"][1], P["br"][1],
                   P["moe_wg"], P["moe_wu"], P["moe_wd"], 1, **kw_moe)
    y = final_norm(x2, W["norm_final"][None, :], tm=tm).reshape(B, T, D)
    return y, out


def kernel(x_prompt, x_sample, c_prompt, c_sample, cache_sb_kv, cache_mla_latent, cache_nsa_cmp_kv, cache_nsa_slc_kv,
           cache_fox_kv, cache_fox_logf, state_nsa_win_kv, page_table, ada_w, ada_b, norm_mix, norm_ffn, norm_final,
           ev_w_in, ev_q_norm, ev_kv_norm, ev_w_uq, ev_w_uk, ev_w_uv, ev_w_out, od_w_in, od_cmp_pe, od_cmp_w, od_fox_fb,
           od_w_out, moe_w_grp, moe_b_grp, moe_w_rt, moe_b_rt, moe_w_gate, moe_w_up, moe_w_down):
    assert ada_w.shape[0] == 2, "two trunk layers (one even, one odd) are implemented"
    W = dict(ada_w=ada_w, ada_b=ada_b, norm_mix=norm_mix, norm_ffn=norm_ffn, norm_final=norm_final, ev_w_in=ev_w_in,
             ev_q_norm=ev_q_norm, ev_kv_norm=ev_kv_norm, ev_w_uq=ev_w_uq, ev_w_uk=ev_w_uk, ev_w_uv=ev_w_uv, ev_w_out=ev_w_out,
             od_w_in=od_w_in, od_cmp_pe=od_cmp_pe, od_cmp_w=od_cmp_w, od_fox_fb=od_fox_fb, od_w_out=od_w_out,
             moe_w_gate=moe_w_gate, moe_w_up=moe_w_up, moe_w_down=moe_w_down)
    P = _prep_weights(W)
    P["wr"] = _pad_last(jnp.concatenate([moe_w_rt, moe_w_grp], axis=-1), LANES)
    P["br"] = _pad_last(jnp.concatenate([moe_b_rt, moe_b_grp], axis=-1), LANES)[:, None, :]
    past = dict(pt=page_table, len=page_table.shape[1] * cache_sb_kv.shape[2], sb=cache_sb_kv[0], mla=cache_mla_latent[0],
                cmp=cache_nsa_cmp_kv[0], slc=cache_nsa_slc_kv[0], fox=cache_fox_kv[0], logf=cache_fox_logf[0],
                win=state_nsa_win_kv[0])
    y_p, sp = _trunk(x_prompt, c_prompt, W, P, None)
    y_s, ss = _trunk(x_sample, c_sample, W, P, past)
    names = ("sb", "mla", "cmp", "slc", "fox", "logf", "win")
    outs = [y_p, y_s]
    for n in names:
        outs += [sp[n], ss[n]]
    return tuple(outs)
```
